```python
import math
import jax, jax.numpy as jnp
from jax import lax
import numpy as np

D_MODEL = 1024
BATCH = 2
SEQ = 16384
DEPTH = 2

PLE_DIM = 256
EPS = 1e-6
GLA_HEADS = 4
GLA_DK = 64
GLA_DV = 128
GLA_RANK = 16
GLA_TAU = 16.0
GLA_CHUNK = 64
DIFF_HEADS = 4
DIFF_DH = 64
Q_BLOCK = 128
CONV_WIDTH = 512
CONV_K = 3
N_GROUPS = 4
EXPERTS_PER_GROUP = 4
N_EXPERTS = N_GROUPS * EXPERTS_PER_GROUP
TOP_K = 2
D_EXPERT = 512
N_BRANCH = 3

GLA_QK = GLA_HEADS * GLA_DK
GLA_V = GLA_HEADS * GLA_DV
DIFF_QK = DIFF_HEADS * 2 * DIFF_DH
DIFF_V = DIFF_HEADS * 2 * DIFF_DH
IN_SIZES = (GLA_QK, GLA_QK, GLA_V, GLA_V, GLA_RANK,
            DIFF_QK, DIFF_QK, DIFF_V,
            CONV_WIDTH, CONV_WIDTH, CONV_WIDTH,
            N_BRANCH * D_MODEL)
IN_TOTAL = sum(IN_SIZES)

kernel_name = "hybrid_gla_diffattn_shortconv_hiermoe"


def rmsnorm(x, g):
    xf = x.astype(jnp.float32)
    y = xf * lax.rsqrt(jnp.mean(xf * xf, axis=-1, keepdims=True) + EPS)
    return (y * g.astype(jnp.float32)).astype(x.dtype)


def gla_chunked(q, k, v, log_a):
    B_, T, H, dk = q.shape
    dv = v.shape[-1]
    n = T // GLA_CHUNK

    def to_chunks(a):
        return a.astype(jnp.float32).reshape(B_, n, GLA_CHUNK, H, a.shape[-1]).transpose(1, 0, 3, 2, 4)

    qc, kc, vc, ac = to_chunks(q), to_chunks(k), to_chunks(v), to_chunks(log_a)
    causal = jnp.tril(jnp.ones((GLA_CHUNK, GLA_CHUNK), dtype=bool))

    def step(S, inp):
        qi, ki, vi, ai = inp
        b = jnp.cumsum(ai, axis=-2)
        o_inter = jnp.einsum('bhtk,bhkv->bhtv', qi * jnp.exp(b), S)
        rel = b[:, :, :, None, :] - b[:, :, None, :, :]
        decay = jnp.exp(jnp.where(causal[:, :, None], rel, -jnp.inf))
        A = jnp.einsum('bhtk,bhsk,bhtsk->bhts', qi, ki, decay)
        o_intra = jnp.einsum('bhts,bhsv->bhtv', A, vi)
        b_last = b[:, :, -1:, :]
        S = jnp.exp(b_last[:, :, 0, :])[..., None] * S + jnp.einsum(
            'bhsk,bhsv->bhkv', ki * jnp.exp(b_last - b), vi)
        return S, o_inter + o_intra

    S0 = jnp.zeros((B_, H, dk, dv), jnp.float32)
    _, o = lax.scan(step, S0, (qc, kc, vc, ac))
    return o.transpose(1, 0, 3, 2, 4).reshape(B_, T, H, dv).astype(v.dtype)


def diff_attention(q, k, v, lam):
    B_, T, H, _, dh = q.shape
    nb = T // Q_BLOCK
    scale = dh ** -0.5
    slopes = 2.0 ** (-8.0 * (jnp.arange(H, dtype=jnp.float32) + 1.0) / H)
    qb = q.reshape(B_, nb, Q_BLOCK, H, 2, dh).transpose(1, 0, 3, 4, 2, 5)
    kt = k.transpose(0, 2, 3, 1, 4)
    vt = v.transpose(0, 2, 1, 3)
    kpos = jnp.arange(T)

    def block(args):
        qi, bi = args
        qpos = bi * Q_BLOCK + jnp.arange(Q_BLOCK)
        dist = qpos[:, None] - kpos[None, :]
        s = jnp.einsum('bhmqd,bhmkd->bhmqk', qi, kt).astype(jnp.float32) * scale
        s = s - slopes[None, :, None, None, None] * dist.astype(jnp.float32)
        s = jnp.where(dist >= 0, s, -jnp.inf)
        a = jax.nn.softmax(s, axis=-1)
        w = a[:, :, 0] - lam * a[:, :, 1]
        return jnp.einsum('bhqk,bhkv->bhqv', w.astype(v.dtype), vt)

    o = lax.map(block, (qb, jnp.arange(nb)))
    return o.transpose(1, 0, 3, 2, 4).reshape(B_, T, H, 2 * dh)


def short_conv(x_in, b_gate, c_gate, w):
    u = c_gate * x_in
    y = lax.conv_general_dilated(
        u, w.astype(u.dtype)[:, None, :], window_strides=(1,), padding=[(CONV_K - 1, 0)],
        dimension_numbers=('NWC', 'WIO', 'NWC'), feature_group_count=u.shape[-1])
    return b_gate * y


def token_mixer(h, layer, w_in, w_gla_a2, b_gla_a, g_gla_norm, w_o_gla, diff_lam,
                g_diff_norm, w_o_diff, conv_w, w_o_conv, w_out):
    B_, T, _ = h.shape
    z = h @ w_in
    points = np.cumsum(np.array(IN_SIZES))[:-1].tolist()
    (gq, gk, gv, gg, ga, dq, dk, dv, cx, cb, cc, gates) = jnp.split(z, points, axis=-1)

    log_a = jax.nn.log_sigmoid((ga @ w_gla_a2 + b_gla_a).astype(jnp.float32)) / GLA_TAU
    o_a = gla_chunked(gq.reshape(B_, T, GLA_HEADS, GLA_DK) * (GLA_DK ** -0.5),
                      gk.reshape(B_, T, GLA_HEADS, GLA_DK),
                      gv.reshape(B_, T, GLA_HEADS, GLA_DV),
                      log_a.reshape(B_, T, GLA_HEADS, GLA_DK))
    o_a = rmsnorm(o_a, g_gla_norm).reshape(B_, T, GLA_V) * jax.nn.silu(gg)
    y_a = o_a @ w_o_gla

    lam_init = 0.8 - 0.6 * math.exp(-0.3 * layer)
    lp = diff_lam.astype(jnp.float32)
    lam = jnp.exp(jnp.sum(lp[0] * lp[1])) - jnp.exp(jnp.sum(lp[2] * lp[3])) + lam_init
    o_b = diff_attention(dq.reshape(B_, T, DIFF_HEADS, 2, DIFF_DH),
                         dk.reshape(B_, T, DIFF_HEADS, 2, DIFF_DH),
                         dv.reshape(B_, T, DIFF_HEADS, 2 * DIFF_DH), lam)
    o_b = rmsnorm(o_b, g_diff_norm) * (1.0 - lam_init)
    y_b = o_b.reshape(B_, T, DIFF_V) @ w_o_diff

    y_c = short_conv(cx, cb, cc, conv_w) @ w_o_conv

    g = jax.nn.sigmoid(gates.reshape(B_, T, N_BRANCH, D_MODEL))
    merged = g[:, :, 0] * y_a + g[:, :, 1] * y_b + g[:, :, 2] * y_c
    return merged @ w_out


def hier_moe(h, w_rg, b_rg, w_re, b_re, w_gu, w_dn):
    B_, T, D = h.shape
    t = h.reshape(-1, D)
    g_prob = jax.nn.softmax((t @ w_rg).astype(jnp.float32) + b_rg.astype(jnp.float32), axis=-1)
    gp, gi = lax.top_k(g_prob, 1)
    e_logits = ((t @ w_re).astype(jnp.float32) + b_re.astype(jnp.float32)).reshape(
        -1, N_GROUPS, EXPERTS_PER_GROUP)
    within = jnp.take_along_axis(e_logits, gi[:, :, None], axis=1)[:, 0]
    ev, ei = lax.top_k(within, TOP_K)
    ew = jax.nn.softmax(ev, axis=-1) * gp
    eidx = gi * EXPERTS_PER_GROUP + ei
    combine = jnp.sum(jax.nn.one_hot(eidx, N_EXPERTS, dtype=jnp.float32) * ew[..., None],
                      axis=1).astype(t.dtype)
    out = jnp.zeros_like(t)
    for e in range(N_EXPERTS):
        gate, up = jnp.split(t @ w_gu[e], 2, axis=-1)
        out = out + combine[:, e:e + 1] * ((jax.nn.silu(gate) * up) @ w_dn[e])
    return out.reshape(B_, T, D)


def setup_inputs(seed: int = 0) -> dict:
    key = jax.random.key(seed)
    ks = jax.random.split(key, 24)
    f32 = jnp.float32
    nrm = lambda k, shape, s: jax.random.normal(k, shape, f32) * s
    gain = lambda k, shape: 1.0 + 0.02 * jax.random.normal(k, shape, f32)
    L = DEPTH
    return {
        "x": jax.random.normal(ks[0], (BATCH, SEQ, D_MODEL), f32),
        "p": jax.random.normal(ks[1], (DEPTH, BATCH, SEQ, PLE_DIM), f32),
        "g_mix": gain(ks[2], (L, D_MODEL)),
        "w_in": nrm(ks[3], (L, D_MODEL, IN_TOTAL), D_MODEL ** -0.5),
        "w_gla_a2": nrm(ks[4], (L, GLA_RANK, GLA_QK), GLA_RANK ** -0.5),
        "b_gla_a": nrm(ks[5], (L, GLA_QK), 0.1),
        "g_gla_norm": gain(ks[6], (L, GLA_DV)),
        "w_o_gla": nrm(ks[7], (L, GLA_V, D_MODEL), GLA_V ** -0.5),
        "diff_lam": nrm(ks[8], (L, 4, DIFF_DH), 0.1),
        "g_diff_norm": gain(ks[9], (L, 2 * DIFF_DH)),
        "w_o_diff": nrm(ks[10], (L, DIFF_V, D_MODEL), DIFF_V ** -0.5),
        "conv_w": nrm(ks[11], (L, CONV_K, CONV_WIDTH), CONV_K ** -0.5),
        "w_o_conv": nrm(ks[12], (L, CONV_WIDTH, D_MODEL), CONV_WIDTH ** -0.5),
        "w_out": nrm(ks[13], (L, D_MODEL, D_MODEL), D_MODEL ** -0.5),
        "g_ffn": gain(ks[14], (L, D_MODEL)),
        "w_router_group": nrm(ks[15], (L, D_MODEL, N_GROUPS), D_MODEL ** -0.5),
        "b_router_group": nrm(ks[16], (L, N_GROUPS), 0.01),
        "w_router_expert": nrm(ks[17], (L, D_MODEL, N_EXPERTS), D_MODEL ** -0.5),
        "b_router_expert": nrm(ks[18], (L, N_EXPERTS), 0.01),
        "w_expert_gate_up": nrm(ks[19], (L, N_EXPERTS, D_MODEL, 2 * D_EXPERT), D_MODEL ** -0.5),
        "w_expert_down": nrm(ks[20], (L, N_EXPERTS, D_EXPERT, D_MODEL), D_EXPERT ** -0.5),
        "w_ple": nrm(ks[21], (L, PLE_DIM, D_MODEL), PLE_DIM ** -0.5),
        "w_ple_gate": nrm(ks[22], (L, D_MODEL, D_MODEL), D_MODEL ** -0.5),
        "g_final": gain(ks[23], (D_MODEL,)),
    }


def reference(x, p, g_mix, w_in, w_gla_a2, b_gla_a, g_gla_norm, w_o_gla, diff_lam,
              g_diff_norm, w_o_diff, conv_w, w_o_conv, w_out, g_ffn, w_router_group,
              b_router_group, w_router_expert, b_router_expert, w_expert_gate_up,
              w_expert_down, w_ple, w_ple_gate, g_final):
    for i in range(DEPTH):
        h = rmsnorm(x, g_mix[i])
        x = x + token_mixer(h, i, w_in[i], w_gla_a2[i], b_gla_a[i], g_gla_norm[i], w_o_gla[i],
                            diff_lam[i], g_diff_norm[i], w_o_diff[i], conv_w[i], w_o_conv[i],
                            w_out[i])
        h = rmsnorm(x, g_ffn[i])
        x = x + hier_moe(h, w_router_group[i], b_router_group[i], w_router_expert[i],
                         b_router_expert[i], w_expert_gate_up[i], w_expert_down[i])
        x = x + jax.nn.sigmoid(x @ w_ple_gate[i]) * (p[i] @ w_ple[i])
    return rmsnorm(x, g_final)
```

```python
import functools
import math

import numpy as np
import jax
import jax.numpy as jnp
from jax import lax
from jax.experimental import pallas as pl
from jax.experimental.pallas import tpu as pltpu

F32 = jnp.float32
BF16 = jnp.bfloat16

EPS = 1e-6
GLA_HEADS = 4
GLA_DK = 64
GLA_DV = 128
GLA_RANK = 16
GLA_TAU = 16.0
GLA_QK = GLA_HEADS * GLA_DK
GLA_V = GLA_HEADS * GLA_DV
DIFF_HEADS = 4
DIFF_DH = 64
DIFF_DV = 2 * DIFF_DH
DIFF_QK = DIFF_HEADS * 2 * DIFF_DH
DIFF_V = DIFF_HEADS * DIFF_DV
CONV_WIDTH = 512
CONV_K = 3
N_GROUPS = 4
EXPERTS_PER_GROUP = 4
N_EXPERTS = N_GROUPS * EXPERTS_PER_GROUP
EPG_SHIFT = EXPERTS_PER_GROUP.bit_length() - 1
assert 1 << EPG_SHIFT == EXPERTS_PER_GROUP
N_PAIRS = EXPERTS_PER_GROUP * (EXPERTS_PER_GROUP - 1) // 2
N_CLASSES = N_GROUPS * N_PAIRS
N_BRANCH = 3

LANES = 128
POS_BASE = 128
N_POS_DIGITS = 3
GLA_SAFE_LOG_DECAY = 60.0

TOKEN_TILE = 512
GLA_TILE = 256
ATTN_TILE = 1024
EXPERT_TILE = 256
META_CLS = N_EXPERTS
META_RANK = N_EXPERTS + 1

VMEM_LIMIT = 56 * 1024 * 1024


def _const_spec(shape):
    nd = len(shape)
    return pl.BlockSpec(shape, lambda *_: (0,) * nd, pipeline_mode=pl.Buffered(1))


def _params(sem):
    return pltpu.CompilerParams(dimension_semantics=sem, vmem_limit_bytes=VMEM_LIMIT)


def _rms(xf, g):
    return xf * lax.rsqrt(jnp.mean(xf * xf, axis=-1, keepdims=True) + EPS) * g


def _sigmoid(v):
    return 1.0 / (1.0 + jnp.exp(-v))


def _dot(a, b):
    return jnp.dot(a, b, preferred_element_type=F32)


def _dot_nt(a, b):
    return lax.dot_general(a, b, (((1,), (1,)), ((), ())), preferred_element_type=F32)


def _dot_tn(a, b):
    return lax.dot_general(a, b, (((0,), (0,)), ((), ())), preferred_element_type=F32)


def _split_bf16(v):
    hi = v.astype(BF16)
    lo = (v - hi.astype(F32)).astype(BF16)
    return hi, lo


def _inproj_main_kernel(x_ref, g_ref, wgla_ref, wga_ref, w2_ref, ba_ref, wconv_ref, cw_ref,
                        wgates_ref, gq_ref, gk_ref, la_ref, gv_ref, sg_ref, yc_ref, gates_ref,
                        ubuf_ref, *, tiles_per_seq, tm):
    i = pl.program_id(0)
    h = _rms(x_ref[...], g_ref[...]).astype(BF16)

    z = _dot(h, wgla_ref[...])
    gq_ref[...] = z[:, :GLA_QK] * (GLA_DK ** -0.5)
    gk_ref[...] = z[:, GLA_QK:2 * GLA_QK]
    gv_ref[...] = z[:, 2 * GLA_QK:2 * GLA_QK + GLA_V].astype(BF16)
    gg = z[:, 2 * GLA_QK + GLA_V:]
    sg_ref[...] = (gg * _sigmoid(gg)).astype(BF16)

    ga = _dot(h, wga_ref[...])
    xa = _dot(ga.astype(BF16), w2_ref[...]) + ba_ref[...]
    log_sig = jnp.minimum(xa, 0.0) - jnp.log(1.0 + jnp.exp(-jnp.abs(xa)))
    la_ref[...] = log_sig * (1.0 / GLA_TAU)

    zc = _dot(h, wconv_ref[...])
    u = zc[:, 2 * CONV_WIDTH:] * zc[:, :CONV_WIDTH]

    @pl.when(i % tiles_per_seq == 0)
    def _():
        ubuf_ref[0:8, :] = jnp.zeros((8, CONV_WIDTH), F32)

    ubuf_ref[8:8 + tm, :] = u
    cw = cw_ref[...]
    y = (cw[0:1, :] * ubuf_ref[6:6 + tm, :] + cw[1:2, :] * ubuf_ref[7:7 + tm, :]
         + cw[2:3, :] * u)
    yc_ref[...] = (zc[:, CONV_WIDTH:2 * CONV_WIDTH] * y).astype(BF16)
    ubuf_ref[0:8, :] = ubuf_ref[tm:tm + 8, :]

    gates_ref[...] = _sigmoid(_dot(h, wgates_ref[...])).astype(BF16)


def _inproj_main(x2d, g, wgla, wga, w2, ba, wconv, cw, wgates, *, seq):
    n, d = x2d.shape
    tm = min(TOKEN_TILE, seq)
    row = lambda w: pl.BlockSpec((tm, w), lambda i: (i, 0))
    out_shape = (
        jax.ShapeDtypeStruct((n, GLA_QK), F32), jax.ShapeDtypeStruct((n, GLA_QK), F32),
        jax.ShapeDtypeStruct((n, GLA_QK), F32), jax.ShapeDtypeStruct((n, GLA_V), BF16),
        jax.ShapeDtypeStruct((n, GLA_V), BF16), jax.ShapeDtypeStruct((n, CONV_WIDTH), BF16),
        jax.ShapeDtypeStruct((n, N_BRANCH * d), BF16))
    return pl.pallas_call(
        functools.partial(_inproj_main_kernel, tiles_per_seq=seq // tm, tm=tm),
        grid=(n // tm,),
        in_specs=[row(d), _const_spec(g.shape), _const_spec(wgla.shape), _const_spec(wga.shape),
                  _const_spec(w2.shape), _const_spec(ba.shape), _const_spec(wconv.shape),
                  _const_spec(cw.shape), _const_spec(wgates.shape)],
        out_specs=(row(GLA_QK), row(GLA_QK), row(GLA_QK), row(GLA_V), row(GLA_V),
                   row(CONV_WIDTH), row(N_BRANCH * d)),
        out_shape=out_shape,
        scratch_shapes=[pltpu.VMEM((8 + tm, CONV_WIDTH), F32)],
        compiler_params=_params(("arbitrary",)),
        name="inproj_main",
    )(x2d, g, wgla, wga, w2, ba, wconv, cw, wgates)


def _inproj_diff_kernel(x_ref, g_ref, wq_ref, qc_ref, wk_ref, pf_ref, wv_ref, vc_ref,
                        q_ref, kt_ref, v_ref):
    h = _rms(x_ref[...], g_ref[...]).astype(BF16)
    q_ref[...] = (_dot(h, wq_ref[...]) * (DIFF_DH ** -0.5) + qc_ref[...]).astype(BF16)
    zk = _dot(h, wk_ref[...])
    pf = pf_ref[...]
    zk = zk + jnp.concatenate([pf] * (zk.shape[1] // LANES), axis=1)
    kt_ref[...] = zk.T.astype(BF16)
    v_ref[...] = (_dot(h, wv_ref[...]) + vc_ref[...]).astype(BF16)


def _inproj_diff(x2d, g, wq, qc, wk, pf, wv, vc, *, batch, seq):
    n, d = x2d.shape
    tm = min(TOKEN_TILE, seq)
    tps = seq // tm
    wq_cols, wk_cols, wv_cols = wq.shape[1], wk.shape[1], wv.shape[1]
    return pl.pallas_call(
        _inproj_diff_kernel,
        grid=(n // tm,),
        in_specs=[pl.BlockSpec((tm, d), lambda i: (i, 0)), _const_spec(g.shape),
                  _const_spec(wq.shape), _const_spec(qc.shape), _const_spec(wk.shape),
                  pl.BlockSpec((tm, LANES), lambda i: (i % tps, 0)),
                  _const_spec(wv.shape), _const_spec(vc.shape)],
        out_specs=(pl.BlockSpec((tm, wq_cols), lambda i: (i, 0)),
                   pl.BlockSpec((None, wk_cols, tm), lambda i: (i // tps, 0, i % tps)),
                   pl.BlockSpec((tm, wv_cols), lambda i: (i, 0))),
        out_shape=(jax.ShapeDtypeStruct((n, wq_cols), BF16),
                   jax.ShapeDtypeStruct((batch, wk_cols, seq), BF16),
                   jax.ShapeDtypeStruct((n, wv_cols), BF16)),
        compiler_params=_params(("arbitrary",)),
        name="inproj_diff",
    )(x2d, g, wq, qc, wk, pf, wv, vc)


def _gla_kernel(q_ref, k_ref, la_ref, v_ref, sg_ref, gn_ref, o_ref, st_ref, a_ref, *, tg):
    @pl.when(pl.program_id(1) == 0)
    def _():
        st_ref[...] = jnp.zeros(st_ref.shape, F32)

    la = la_ref[...]
    la_hi, la_lo = _split_bf16(la)
    row = lax.broadcasted_iota(jnp.int32, (tg, tg), 0)
    col = lax.broadcasted_iota(jnp.int32, (tg, tg), 1)
    causal = col <= row

    def prefix(sel):
        sel = jnp.where(sel, 1.0, 0.0).astype(BF16)
        return _dot(sel, la_hi) + _dot(sel, la_lo)

    b = prefix(causal)
    btot = b[tg - 1:tg, :]
    q = q_ref[...]
    k = k_ref[...]
    qe = (q * jnp.exp(b)).astype(BF16)
    kd = (k * jnp.exp(btot - b)).astype(BF16)

    safe = jnp.min(btot) >= -GLA_SAFE_LOG_DECAY

    @pl.when(safe)
    def _():
        ke = (k * jnp.exp(-b)).astype(BF16)
        for hd in range(GLA_HEADS):
            sl = slice(hd * GLA_DK, (hd + 1) * GLA_DK)
            a = _dot_nt(qe[:, sl], ke[:, sl])
            a_ref[hd] = jnp.where(causal, a, 0.0).astype(BF16)

    @pl.when(jnp.logical_not(safe))
    def _():
        qb = q.astype(BF16)
        kb = k.astype(BF16)
        acc = [jnp.where(row == col, _dot_nt(qb[:, hd * GLA_DK:(hd + 1) * GLA_DK],
                                             kb[:, hd * GLA_DK:(hd + 1) * GLA_DK]), 0.0)
               for hd in range(GLA_HEADS)]
        rowv = lax.broadcasted_iota(jnp.int32, (tg, 1), 0)
        half = tg // 2
        while half >= 1:
            shift = int(math.log2(2 * half))
            mid = ((row >> shift) << shift) + (half - 1)
            r = prefix(col <= mid)
            second = (rowv & (2 * half - 1)) >= half
            fac = jnp.exp(jnp.where(second, b - r, r - b))
            ql = jnp.where(second, q * fac, 0.0).astype(BF16)
            kl = jnp.where(second, 0.0, k * fac).astype(BF16)
            same = (row >> shift) == (col >> shift)
            for hd in range(GLA_HEADS):
                sl = slice(hd * GLA_DK, (hd + 1) * GLA_DK)
                acc[hd] = acc[hd] + jnp.where(same, _dot_nt(ql[:, sl], kl[:, sl]), 0.0)
            half //= 2
        for hd in range(GLA_HEADS):
            a_ref[hd] = acc[hd].astype(BF16)

    v = v_ref[...]
    gn = gn_ref[...]
    decay = jnp.exp(btot)
    for hd in range(GLA_HEADS):
        sl = slice(hd * GLA_DK, (hd + 1) * GLA_DK)
        vs = slice(hd * GLA_DV, (hd + 1) * GLA_DV)
        st = st_ref[hd]
        o = _dot(a_ref[hd], v[:, vs]) + _dot_nt(qe[:, sl], st.astype(BF16))
        st_ref[hd] = st * decay[:, sl] + _dot_tn(v[:, vs], kd[:, sl])
        o_ref[:, vs] = (_rms(o, gn) * sg_ref[:, vs].astype(F32)).astype(BF16)


def _gla(gq, gk, la, gv, sg, gn, *, batch, seq):
    n = gq.shape[0]
    tg = min(GLA_TILE, seq)
    tps = seq // tg
    row = lambda w: pl.BlockSpec((tg, w), lambda b, j: (b * tps + j, 0))
    return pl.pallas_call(
        functools.partial(_gla_kernel, tg=tg),
        grid=(batch, tps),
        in_specs=[row(GLA_QK), row(GLA_QK), row(GLA_QK), row(GLA_V), row(GLA_V),
                  pl.BlockSpec(gn.shape, lambda b, j: (0, 0))],
        out_specs=row(GLA_V),
        out_shape=jax.ShapeDtypeStruct((n, GLA_V), BF16),
        scratch_shapes=[pltpu.VMEM((GLA_HEADS, GLA_DV, GLA_DK), F32),
                        pltpu.VMEM((GLA_HEADS, tg, tg), BF16)],
        compiler_params=_params(("arbitrary", "arbitrary")),
        name="gla",
    )(gq, gk, la, gv, sg, gn)


def _attn_kernel(qi_ref, kj_ref, q_ref, kt_ref, v_ref, lam_ref, gn_ref, o_ref, m_ref, acc_ref,
                 *, lam_init, ta):
    t = pl.program_id(2)
    qi = qi_ref[t]
    kj = kj_ref[t]

    @pl.when(kj == 0)
    def _():
        m_ref[...] = jnp.full(m_ref.shape, -jnp.inf, F32)
        acc_ref[...] = jnp.zeros(acc_ref.shape, F32)

    def step(masked):
        v = v_ref[...]
        for mp in range(2):
            s = _dot(q_ref[:, mp * LANES:(mp + 1) * LANES], kt_ref[mp * LANES:(mp + 1) * LANES, :])
            if masked:
                row = lax.broadcasted_iota(jnp.int32, (ta, ta), 0)
                col = lax.broadcasted_iota(jnp.int32, (ta, ta), 1)
                s = jnp.where(col <= row, s, -jnp.inf)
            m_old = m_ref[mp]
            m_new = jnp.maximum(m_old, jnp.max(s, axis=1, keepdims=True))
            p = jnp.exp(s - m_new).astype(BF16)
            acc_ref[mp] = jnp.exp(m_old - m_new) * acc_ref[mp] + _dot(p, v)
            m_ref[mp] = m_new

    @pl.when(kj < qi)
    def _():
        step(False)

    @pl.when(kj == qi)
    def _():
        step(True)
        lp = lam_ref[...]
        lam = (jnp.exp(jnp.sum(lp[0:1] * lp[1:2], axis=1, keepdims=True))
               - jnp.exp(jnp.sum(lp[2:3] * lp[3:4], axis=1, keepdims=True)) + lam_init)
        a1 = acc_ref[0]
        a2 = acc_ref[1]
        o = (a1[:, :DIFF_DV] / a1[:, DIFF_DV:DIFF_DV + 1]
             - lam * (a2[:, :DIFF_DV] / a2[:, DIFF_DV:DIFF_DV + 1]))
        o_ref[...] = (_rms(o, gn_ref[...]) * (1.0 - lam_init)).astype(BF16)


def _attn(dq, dkt, dv, lam_p, gn, *, batch, seq, lam_init):
    ta = min(ATTN_TILE, seq)
    nq = seq // ta
    pairs = [(i, j) for i in range(nq) for j in range(i + 1)]
    qi_tab = jnp.asarray(np.array([p[0] for p in pairs], np.int32))
    kj_tab = jnp.asarray(np.array([p[1] for p in pairs], np.int32))
    dq3 = dq.reshape(batch, seq, dq.shape[1])
    dv3 = dv.reshape(batch, seq, dv.shape[1])
    grid_spec = pltpu.PrefetchScalarGridSpec(
        num_scalar_prefetch=2,
        grid=(batch, DIFF_HEADS, len(pairs)),
        in_specs=[
            pl.BlockSpec((None, ta, 2 * LANES), lambda b, h, t, qi, kj: (b, qi[t], h)),
            pl.BlockSpec((None, 2 * LANES, ta), lambda b, h, t, qi, kj: (b, h, kj[t])),
            pl.BlockSpec((None, ta, 2 * LANES), lambda b, h, t, qi, kj: (b, kj[t], h)),
            pl.BlockSpec(lam_p.shape, lambda b, h, t, qi, kj: (0, 0)),
            pl.BlockSpec(gn.shape, lambda b, h, t, qi, kj: (0, 0)),
        ],
        out_specs=pl.BlockSpec((None, ta, DIFF_DV), lambda b, h, t, qi, kj: (b, qi[t], h)),
        scratch_shapes=[pltpu.VMEM((2, ta, 1), F32), pltpu.VMEM((2, ta, 2 * LANES), F32)],
    )
    out = pl.pallas_call(
        functools.partial(_attn_kernel, lam_init=lam_init, ta=ta),
        grid_spec=grid_spec,
        out_shape=jax.ShapeDtypeStruct((batch, seq, DIFF_V), BF16),
        compiler_params=_params(("arbitrary", "arbitrary", "arbitrary")),
        name="diff_attn",
    )(qi_tab, kj_tab, dq3, dkt, dv3, lam_p, gn)
    return out.reshape(batch * seq, DIFF_V)


def _merge_route_kernel(x_ref, oa_ref, ob_ref, oc_ref, gt_ref, woa_ref, wob_ref, woc_ref, wout_ref,
                        gf_ref, wrh_ref, wrl_ref, br_ref, x1_ref, hx_ref, cnt_ref, *, tm, d):
    i = pl.program_id(0)

    @pl.when(i == 0)
    def _():
        cnt_ref[...] = jnp.zeros(cnt_ref.shape, F32)

    ya = _dot(oa_ref[...], woa_ref[...])
    yb = _dot(ob_ref[...], wob_ref[...])
    yc = _dot(oc_ref[...], woc_ref[...])
    merged = (gt_ref[:, 0:d].astype(F32) * ya + gt_ref[:, d:2 * d].astype(F32) * yb
              + gt_ref[:, 2 * d:3 * d].astype(F32) * yc)
    x1 = x_ref[...] + _dot(merged.astype(BF16), wout_ref[...])
    x1_ref[...] = x1
    h2 = _rms(x1, gf_ref[...])
    hx_ref[:, 0:d] = h2

    h_hi, h_lo = _split_bf16(h2)
    wrh = wrh_ref[...]
    logits = _dot(h_hi, wrh) + _dot(h_lo, wrh) + _dot(h_hi, wrl_ref[...]) + br_ref[...]
    lane = lax.broadcasted_iota(jnp.int32, (tm, LANES), 1)
    ninf = -jnp.inf
    glog = jnp.where(lane < N_GROUPS, logits, ninf)
    gmax = jnp.max(glog, axis=1, keepdims=True)
    gi = jnp.min(jnp.where(glog == gmax, lane, LANES), axis=1, keepdims=True)
    gp = 1.0 / jnp.sum(jnp.exp(glog - gmax), axis=1, keepdims=True)
    eid = lane - N_GROUPS
    in_group = (eid >= 0) & (eid < N_EXPERTS) & ((eid >> EPG_SHIFT) == gi)
    w0 = jnp.where(in_group, logits, ninf)
    m1 = jnp.max(w0, axis=1, keepdims=True)
    i1 = jnp.min(jnp.where(w0 == m1, lane, LANES), axis=1, keepdims=True)
    w1 = jnp.where(lane == i1, ninf, w0)
    m2 = jnp.max(w1, axis=1, keepdims=True)
    i2 = jnp.min(jnp.where(w1 == m2, lane, LANES), axis=1, keepdims=True)
    e1 = i1 - N_GROUPS
    e2 = i2 - N_GROUPS
    tt = jnp.exp(m2 - m1)
    p1 = gp / (1.0 + tt)
    p2 = gp * tt / (1.0 + tt)
    comb = jnp.where(lane == e1, p1, 0.0) + jnp.where(lane == e2, p2, 0.0)

    lo = jnp.minimum(e1, e2) & (EXPERTS_PER_GROUP - 1)
    hi = jnp.maximum(e1, e2) & (EXPERTS_PER_GROUP - 1)
    pair = ((lo * (2 * EXPERTS_PER_GROUP - 1 - lo)) >> 1) + (hi - lo - 1)
    cls = gi * N_PAIRS + pair
    onehot = lane == cls
    row = lax.broadcasted_iota(jnp.int32, (tm, tm), 0)
    col = lax.broadcasted_iota(jnp.int32, (tm, tm), 1)
    before = jnp.where(col < row, 1.0, 0.0).astype(BF16)
    seen = _dot(before, jnp.where(onehot, 1.0, 0.0).astype(BF16)) + cnt_ref[0:1, :]
    rank = jnp.sum(jnp.where(onehot, seen, 0.0), axis=1, keepdims=True)
    cnt_ref[0:1, :] = cnt_ref[0:1, :] + jnp.sum(jnp.where(onehot, 1.0, 0.0), axis=0, keepdims=True)

    meta = comb + jnp.where(lane == META_CLS, cls.astype(F32), 0.0) \
        + jnp.where(lane == META_RANK, rank, 0.0)
    hx_ref[:, d:d + LANES] = meta


def _merge_route(x2d, oa, ob, oc, gates, woa, wob, woc, wout, gf, wrh, wrl, br, *, seq):
    n, d = x2d.shape
    tm = min(TOKEN_TILE, seq)
    row = lambda w: pl.BlockSpec((tm, w), lambda i: (i, 0))
    return pl.pallas_call(
        functools.partial(_merge_route_kernel, tm=tm, d=d),
        grid=(n // tm,),
        in_specs=[row(d), row(GLA_V), row(DIFF_V), row(CONV_WIDTH), row(N_BRANCH * d),
                  _const_spec(woa.shape), _const_spec(wob.shape), _const_spec(woc.shape),
                  _const_spec(wout.shape), _const_spec(gf.shape), _const_spec(wrh.shape),
                  _const_spec(wrl.shape), _const_spec(br.shape)],
        out_specs=(row(d), row(d + LANES), pl.BlockSpec((8, LANES), lambda i: (0, 0))),
        out_shape=(jax.ShapeDtypeStruct((n, d), F32), jax.ShapeDtypeStruct((n, d + LANES), F32),
                   jax.ShapeDtypeStruct((8, LANES), F32)),
        compiler_params=_params(("arbitrary",)),
        name="merge_route",
    )(x2d, oa, ob, oc, gates, woa, wob, woc, wout, gf, wrh, wrl, br)


def _expert_kernel(e1_ref, e2_ref, nv_ref, src_ref, hx_ref, wgu1_ref, wgu2_ref, wdn1_ref, wdn2_ref,
                   y_ref, xbuf_ref, ybuf_ref, sem_ref, *, te, d, dexp):
    j = pl.program_id(0)

    @pl.when(j < nv_ref[0])
    def _():
        base = j * te

        def gather_copy(r, tok):
            return pltpu.make_async_copy(hx_ref.at[pl.ds(tok, 1)], xbuf_ref.at[pl.ds(r, 1)],
                                         sem_ref.at[0])

        def scatter_copy(r, tok):
            return pltpu.make_async_copy(ybuf_ref.at[pl.ds(r, 1)], y_ref.at[pl.ds(tok, 1)],
                                         sem_ref.at[1])

        def g_start(r, c):
            gather_copy(r, jnp.maximum(src_ref[base + r], 0)).start()
            return c

        def g_wait(r, c):
            gather_copy(r, 0).wait()
            return c

        lax.fori_loop(0, te, g_start, 0)
        lax.fori_loop(0, te, g_wait, 0)

        xs = xbuf_ref[:, 0:d].astype(BF16)
        meta = xbuf_ref[:, d:d + LANES]
        lane = lax.broadcasted_iota(jnp.int32, (te, LANES), 1)
        out = jnp.zeros((te, d), F32)
        for e_ref, wgu_ref, wdn_ref in ((e1_ref, wgu1_ref, wdn1_ref), (e2_ref, wgu2_ref, wdn2_ref)):
            gu = _dot(xs, wgu_ref[...])
            gate = gu[:, :dexp]
            act = (gate * _sigmoid(gate) * gu[:, dexp:]).astype(BF16)
            wcol = jnp.sum(jnp.where(lane == e_ref[j], meta, 0.0), axis=1, keepdims=True)
            out = out + wcol * _dot(act, wdn_ref[...])
        ybuf_ref[...] = out

        def s_start(r, c):
            tok = src_ref[base + r]

            @pl.when(tok >= 0)
            def _():
                scatter_copy(r, tok).start()
            return c

        def s_wait(r, c):
            @pl.when(src_ref[base + r] >= 0)
            def _():
                scatter_copy(r, 0).wait()
            return c

        lax.fori_loop(0, te, s_start, 0)
        lax.fori_loop(0, te, s_wait, 0)


def _experts(hx, wgu, wdn, e1, e2, nvalid, src, *, te, n_tiles):
    n, dx = hx.shape
    d = dx - LANES
    dexp = wdn.shape[1]
    grid_spec = pltpu.PrefetchScalarGridSpec(
        num_scalar_prefetch=4,
        grid=(n_tiles,),
        in_specs=[
            pl.BlockSpec(memory_space=pl.ANY),
            pl.BlockSpec((None, d, 2 * dexp), lambda j, e1, e2, nv, src: (e1[j], 0, 0)),
            pl.BlockSpec((None, d, 2 * dexp), lambda j, e1, e2, nv, src: (e2[j], 0, 0)),
            pl.BlockSpec((None, dexp, d), lambda j, e1, e2, nv, src: (e1[j], 0, 0)),
            pl.BlockSpec((None, dexp, d), lambda j, e1, e2, nv, src: (e2[j], 0, 0)),
        ],
        out_specs=pl.BlockSpec(memory_space=pl.ANY),
        scratch_shapes=[pltpu.VMEM((te, dx), F32), pltpu.VMEM((te, d), F32),
                        pltpu.SemaphoreType.DMA((2,))],
    )
    return pl.pallas_call(
        functools.partial(_expert_kernel, te=te, d=d, dexp=dexp),
        grid_spec=grid_spec,
        out_shape=jax.ShapeDtypeStruct((n, d), F32),
        compiler_params=_params(("arbitrary",)),
        name="experts",
    )(e1, e2, nvalid, src, hx, wgu, wgu, wdn, wdn)


_PAIR_LO = np.array([lo for lo in range(EXPERTS_PER_GROUP) for hi in range(lo + 1, EXPERTS_PER_GROUP)],
                    np.int32)
_PAIR_HI = np.array([hi for lo in range(EXPERTS_PER_GROUP) for hi in range(lo + 1, EXPERTS_PER_GROUP)],
                    np.int32)


def _dispatch_tables(hx, cnt, *, te, n_tiles):
    n = hx.shape[0]
    d = hx.shape[1] - LANES
    cls = hx[:, d + META_CLS].astype(jnp.int32)
    rank = hx[:, d + META_RANK].astype(jnp.int32)
    counts = cnt[0, :N_CLASSES].astype(jnp.int32)
    tiles_c = (counts + te - 1) // te
    tile_end = jnp.cumsum(tiles_c)
    tile_start = tile_end - tiles_c
    dest = tile_start[cls] * te + rank
    src = jnp.full((n_tiles * te,), -1, jnp.int32).at[dest].set(jnp.arange(n, dtype=jnp.int32))
    tcls = jnp.minimum(jnp.searchsorted(tile_end, jnp.arange(n_tiles, dtype=jnp.int32), side="right"),
                       N_CLASSES - 1).astype(jnp.int32)
    grp = tcls // N_PAIRS
    pr = tcls % N_PAIRS
    e1 = grp * EXPERTS_PER_GROUP + jnp.asarray(_PAIR_LO)[pr]
    e2 = grp * EXPERTS_PER_GROUP + jnp.asarray(_PAIR_HI)[pr]
    return e1.astype(jnp.int32), e2.astype(jnp.int32), tile_end[-1:].astype(jnp.int32), src


def _ple_kernel(x1_ref, y_ref, p_ref, wpg_ref, wple_ref, gfin_ref, o_ref, *, final):
    x2 = x1_ref[...] + y_ref[...]
    gate = _sigmoid(_dot(x2.astype(BF16), wpg_ref[...]))
    x3 = x2 + gate * _dot(p_ref[...].astype(BF16), wple_ref[...])
    o_ref[...] = _rms(x3, gfin_ref[...]) if final else x3


def _ple(x1, y, p2d, wpg, wple, gfin, *, seq, final):
    n, d = x1.shape
    tm = min(TOKEN_TILE, seq)
    row = lambda w: pl.BlockSpec((tm, w), lambda i: (i, 0))
    return pl.pallas_call(
        functools.partial(_ple_kernel, final=final),
        grid=(n // tm,),
        in_specs=[row(d), row(d), row(p2d.shape[1]), _const_spec(wpg.shape), _const_spec(wple.shape),
                  _const_spec(gfin.shape)],
        out_specs=row(d),
        out_shape=jax.ShapeDtypeStruct((n, d), F32),
        compiler_params=_params(("arbitrary",)),
        name="ple",
    )(x1, y, p2d, wpg, wple, gfin)


def _pad_cols(w, width):
    return jnp.pad(w, ((0, 0), (0, width - w.shape[1])))


def _diff_weights(w_dq, w_dk, w_dv):
    d = w_dq.shape[0]
    z64 = jnp.zeros((d, LANES - DIFF_DH), w_dq.dtype)
    z128 = jnp.zeros((d, LANES), w_dq.dtype)
    wq, wk, wv = [], [], []
    for hm in range(2 * DIFF_HEADS):
        wq += [w_dq[:, hm * DIFF_DH:(hm + 1) * DIFF_DH], z64]
        wk += [w_dk[:, hm * DIFF_DH:(hm + 1) * DIFF_DH], z64]
    for hd in range(DIFF_HEADS):
        wv += [w_dv[:, hd * DIFF_DV:(hd + 1) * DIFF_DV], z128]
    cat = lambda parts: jnp.concatenate(parts, axis=1).astype(BF16)
    return cat(wq), cat(wk), cat(wv)


def _diff_constants(seq):
    slopes = [2.0 ** (-8.0 * (hd + 1.0) / DIFF_HEADS) for hd in range(DIFF_HEADS)]
    qc = np.zeros((1, 2 * DIFF_HEADS * LANES), np.float32)
    for hd in range(DIFF_HEADS):
        for mp in range(2):
            for dgt in range(N_POS_DIGITS):
                coef = slopes[hd] * float(POS_BASE ** dgt)
                mant = math.frexp(coef)[0] * 256.0
                assert mant == int(mant), "ALiBi coefficient must be exact in bf16"
                qc[0, (2 * hd + mp) * LANES + DIFF_DH + dgt] = coef
    assert seq <= POS_BASE ** N_POS_DIGITS
    pos = np.arange(seq)
    pf = np.zeros((seq, LANES), np.float32)
    for dgt in range(N_POS_DIGITS):
        pf[:, DIFF_DH + dgt] = (pos // (POS_BASE ** dgt)) % POS_BASE
    vc = np.zeros((1, DIFF_HEADS * 2 * LANES), np.float32)
    for hd in range(DIFF_HEADS):
        vc[0, hd * 2 * LANES + DIFF_DV] = 1.0
    return jnp.asarray(qc), jnp.asarray(pf), jnp.asarray(vc)


def kernel(x, p, g_mix, w_in, w_gla_a2, b_gla_a, g_gla_norm, w_o_gla, diff_lam, g_diff_norm, w_o_diff,
           conv_w, w_o_conv, w_out, g_ffn, w_router_group, b_router_group, w_router_expert,
           b_router_expert, w_expert_gate_up, w_expert_down, w_ple, w_ple_gate, g_final):
    batch, seq, d = x.shape
    depth = w_in.shape[0]
    n = batch * seq
    te = min(EXPERT_TILE, seq)
    n_tiles = n // te + N_CLASSES
    qc, pf, vc = _diff_constants(seq)

    sizes = (GLA_QK, GLA_QK, GLA_V, GLA_V, GLA_RANK, DIFF_QK, DIFF_QK, DIFF_V,
             CONV_WIDTH, CONV_WIDTH, CONV_WIDTH, N_BRANCH * d)
    offs = np.concatenate([[0], np.cumsum(sizes)]).tolist()
    assert offs[-1] == w_in.shape[2]

    x2d = x.reshape(n, d)
    for layer in range(depth):
        wi = w_in[layer]
        seg = lambda a, b: wi[:, offs[a]:offs[b]]
        wgla = seg(0, 4).astype(BF16)
        wga = _pad_cols(seg(4, 5), LANES).astype(BF16)
        w2 = jnp.pad(w_gla_a2[layer], ((0, LANES - GLA_RANK), (0, 0))).astype(BF16)
        wq, wk, wv = _diff_weights(seg(5, 6), seg(6, 7), seg(7, 8))
        wconv = seg(8, 11).astype(BF16)
        wgates = seg(11, 12).astype(BF16)
        lam_init = 0.8 - 0.6 * math.exp(-0.3 * layer)

        gq, gk, la, gv, sg, yc, gates = _inproj_main(
            x2d, g_mix[layer][None, :], wgla, wga, w2, b_gla_a[layer][None, :], wconv,
            conv_w[layer], wgates, seq=seq)
        dq, dkt, dv = _inproj_diff(x2d, g_mix[layer][None, :], wq, qc, wk, pf, wv, vc,
                                   batch=batch, seq=seq)
        oa = _gla(gq, gk, la, gv, sg, g_gla_norm[layer][None, :], batch=batch, seq=seq)
        ob = _attn(dq, dkt, dv, diff_lam[layer], g_diff_norm[layer][None, :],
                   batch=batch, seq=seq, lam_init=lam_init)

        wr = _pad_cols(jnp.concatenate([w_router_group[layer], w_router_expert[layer]], axis=1), LANES)
        wrh = wr.astype(BF16)
        wrl = (wr - wrh.astype(F32)).astype(BF16)
        br = _pad_cols(jnp.concatenate([b_router_group[layer], b_router_expert[layer]])[None, :], LANES)
        x1, hx, cnt = _merge_route(
            x2d, oa, ob, yc, gates, w_o_gla[layer].astype(BF16), w_o_diff[layer].astype(BF16),
            w_o_conv[layer].astype(BF16), w_out[layer].astype(BF16), g_ffn[layer][None, :],
            wrh, wrl, br, seq=seq)

        e1, e2, nvalid, src = _dispatch_tables(hx, cnt, te=te, n_tiles=n_tiles)
        y = _experts(hx, w_expert_gate_up[layer].astype(BF16), w_expert_down[layer].astype(BF16),
                     e1, e2, nvalid, src, te=te, n_tiles=n_tiles)

        x2d = _ple(x1, y, p[layer].reshape(n, p.shape[-1]), w_ple_gate[layer].astype(BF16),
                   w_ple[layer].astype(BF16), g_final[None, :], seq=seq,
                   final=(layer == depth - 1))
    return x2d.reshape(batch, seq, d)
```

```python
import functools
import math

import numpy as np
import jax
import jax.numpy as jnp
from jax import lax
from jax.experimental import pallas as pl
from jax.experimental.pallas import tpu as pltpu

F32 = jnp.float32
BF16 = jnp.bfloat16

EPS = 1e-6
GLA_HEADS = 4
GLA_DK = 64
GLA_DV = 128
GLA_RANK = 16
GLA_TAU = 16.0
GLA_QK = GLA_HEADS * GLA_DK
GLA_V = GLA_HEADS * GLA_DV
DIFF_HEADS = 4
DIFF_DH = 64
DIFF_DV = 2 * DIFF_DH
DIFF_QK = DIFF_HEADS * 2 * DIFF_DH
DIFF_V = DIFF_HEADS * DIFF_DV
CONV_WIDTH = 512
CONV_K = 3
N_GROUPS = 4
EXPERTS_PER_GROUP = 4
N_EXPERTS = N_GROUPS * EXPERTS_PER_GROUP
EPG_SHIFT = EXPERTS_PER_GROUP.bit_length() - 1
assert 1 << EPG_SHIFT == EXPERTS_PER_GROUP
N_PAIRS = EXPERTS_PER_GROUP * (EXPERTS_PER_GROUP - 1) // 2
N_CLASSES = N_GROUPS * N_PAIRS
N_BRANCH = 3

LANES = 128
POS_BASE = 128
N_POS_DIGITS = 3
GLA_SAFE_LOG_DECAY = 60.0

TOKEN_TILE = 512
GLA_TILE = 256
ATTN_Q_TILE = 2048
ATTN_K_TILE = 1024
ATTN_ROWS = 128
MXU_COLS = 256
ATTN_SKIP_MARGIN = 110.0
EXPERT_TILE = 256
META_CLS = N_EXPERTS
META_RANK = N_EXPERTS + 1

VMEM_LIMIT = 56 * 1024 * 1024


def _const_spec(shape):
    nd = len(shape)
    return pl.BlockSpec(shape, lambda *_: (0,) * nd, pipeline_mode=pl.Buffered(1))


def _params(sem):
    return pltpu.CompilerParams(dimension_semantics=sem, vmem_limit_bytes=VMEM_LIMIT)


def _rms(xf, g):
    return xf * lax.rsqrt(jnp.mean(xf * xf, axis=-1, keepdims=True) + EPS) * g


def _sigmoid(v):
    return 1.0 / (1.0 + jnp.exp(-v))


def _dot(a, b):
    return jnp.dot(a, b, preferred_element_type=F32)


def _dot_nt(a, b):
    return lax.dot_general(a, b, (((1,), (1,)), ((), ())), preferred_element_type=F32)


def _dot_tn(a, b):
    return lax.dot_general(a, b, (((0,), (0,)), ((), ())), preferred_element_type=F32)


def _split_bf16(v):
    hi = v.astype(BF16)
    lo = (v - hi.astype(F32)).astype(BF16)
    return hi, lo


def _inproj_main_kernel(x_ref, g_ref, wgla_ref, wga_ref, w2_ref, ba_ref, wconv_ref, cw_ref,
                        wgates_ref, gq_ref, gk_ref, la_ref, gv_ref, sg_ref, yc_ref, gates_ref,
                        ubuf_ref, *, tiles_per_seq, tm):
    i = pl.program_id(0)
    h = _rms(x_ref[...], g_ref[...]).astype(BF16)

    z = _dot(h, wgla_ref[...])
    gq_ref[...] = z[:, :GLA_QK] * (GLA_DK ** -0.5)
    gk_ref[...] = z[:, GLA_QK:2 * GLA_QK]
    gv_ref[...] = z[:, 2 * GLA_QK:2 * GLA_QK + GLA_V].astype(BF16)
    gg = z[:, 2 * GLA_QK + GLA_V:]
    sg_ref[...] = (gg * _sigmoid(gg)).astype(BF16)

    ga = _dot(h, wga_ref[...])
    xa = _dot(ga.astype(BF16), w2_ref[...]) + ba_ref[...]
    log_sig = jnp.minimum(xa, 0.0) - jnp.log(1.0 + jnp.exp(-jnp.abs(xa)))
    la_ref[...] = log_sig * (1.0 / GLA_TAU)

    zc = _dot(h, wconv_ref[...])
    u = zc[:, 2 * CONV_WIDTH:] * zc[:, :CONV_WIDTH]

    @pl.when(i % tiles_per_seq == 0)
    def _():
        ubuf_ref[0:8, :] = jnp.zeros((8, CONV_WIDTH), F32)

    ubuf_ref[8:8 + tm, :] = u
    cw = cw_ref[...]
    y = (cw[0:1, :] * ubuf_ref[6:6 + tm, :] + cw[1:2, :] * ubuf_ref[7:7 + tm, :]
         + cw[2:3, :] * u)
    yc_ref[...] = (zc[:, CONV_WIDTH:2 * CONV_WIDTH] * y).astype(BF16)
    ubuf_ref[0:8, :] = ubuf_ref[tm:tm + 8, :]

    gates_ref[...] = _sigmoid(_dot(h, wgates_ref[...])).astype(BF16)


def _inproj_main(x2d, g, wgla, wga, w2, ba, wconv, cw, wgates, *, seq):
    n, d = x2d.shape
    tm = min(TOKEN_TILE, seq)
    row = lambda w: pl.BlockSpec((tm, w), lambda i: (i, 0))
    out_shape = (
        jax.ShapeDtypeStruct((n, GLA_QK), F32), jax.ShapeDtypeStruct((n, GLA_QK), F32),
        jax.ShapeDtypeStruct((n, GLA_QK), F32), jax.ShapeDtypeStruct((n, GLA_V), BF16),
        jax.ShapeDtypeStruct((n, GLA_V), BF16), jax.ShapeDtypeStruct((n, CONV_WIDTH), BF16),
        jax.ShapeDtypeStruct((n, N_BRANCH * d), BF16))
    return pl.pallas_call(
        functools.partial(_inproj_main_kernel, tiles_per_seq=seq // tm, tm=tm),
        grid=(n // tm,),
        in_specs=[row(d), _const_spec(g.shape), _const_spec(wgla.shape), _const_spec(wga.shape),
                  _const_spec(w2.shape), _const_spec(ba.shape), _const_spec(wconv.shape),
                  _const_spec(cw.shape), _const_spec(wgates.shape)],
        out_specs=(row(GLA_QK), row(GLA_QK), row(GLA_QK), row(GLA_V), row(GLA_V),
                   row(CONV_WIDTH), row(N_BRANCH * d)),
        out_shape=out_shape,
        scratch_shapes=[pltpu.VMEM((8 + tm, CONV_WIDTH), F32)],
        compiler_params=_params(("arbitrary",)),
        name="inproj_main",
    )(x2d, g, wgla, wga, w2, ba, wconv, cw, wgates)


def _inproj_diff_kernel(x_ref, g_ref, wq_ref, qc_ref, wk_ref, pf_ref, wv_ref, vc_ref, iqk_ref, iall_ref,
                        q_ref, kt_ref, v_ref, st_ref):
    h = _rms(x_ref[...], g_ref[...]).astype(BF16)
    qa = (_dot(h, wq_ref[...]) * (DIFF_DH ** -0.5) + qc_ref[...]).astype(BF16)
    q_ref[...] = qa
    zk = _dot(h, wk_ref[...])
    pf = pf_ref[...]
    zk = zk + jnp.concatenate([pf] * (zk.shape[1] // LANES), axis=1)
    kt_ref[...] = zk.T.astype(BF16)
    v_ref[...] = (_dot(h, wv_ref[...]) + vc_ref[...]).astype(BF16)

    qf = qa.astype(F32)
    kf = zk.astype(BF16).astype(F32)

    def block_sums(vals, ind_ref):
        hi, lo = _split_bf16(vals)
        return _dot(hi, ind_ref[...]) + _dot(lo, ind_ref[...])

    qn2 = jnp.max(block_sums(qf * qf, iqk_ref), axis=0, keepdims=True)
    kn2 = jnp.max(block_sums(kf * kf, iqk_ref), axis=0, keepdims=True)
    diag = jnp.min(block_sums(qf * kf, iall_ref), axis=0, keepdims=True)
    st_ref[...] = jnp.concatenate([qn2, kn2, diag, jnp.zeros((5, LANES), F32)], axis=0)


def _inproj_diff(x2d, g, wq, qc, wk, pf, wv, vc, iqk, iall, *, batch, seq):
    n, d = x2d.shape
    tm = min(TOKEN_TILE, seq)
    tps = seq // tm
    wq_cols, wk_cols, wv_cols = wq.shape[1], wk.shape[1], wv.shape[1]
    return pl.pallas_call(
        _inproj_diff_kernel,
        grid=(n // tm,),
        in_specs=[pl.BlockSpec((tm, d), lambda i: (i, 0)), _const_spec(g.shape),
                  _const_spec(wq.shape), _const_spec(qc.shape), _const_spec(wk.shape),
                  pl.BlockSpec((tm, LANES), lambda i: (i % tps, 0)),
                  _const_spec(wv.shape), _const_spec(vc.shape), _const_spec(iqk.shape),
                  _const_spec(iall.shape)],
        out_specs=(pl.BlockSpec((tm, wq_cols), lambda i: (i, 0)),
                   pl.BlockSpec((None, wk_cols, tm), lambda i: (i // tps, 0, i % tps)),
                   pl.BlockSpec((tm, wv_cols), lambda i: (i, 0)),
                   pl.BlockSpec((None, 8, LANES), lambda i: (i, 0, 0))),
        out_shape=(jax.ShapeDtypeStruct((n, wq_cols), BF16),
                   jax.ShapeDtypeStruct((batch, wk_cols, seq), BF16),
                   jax.ShapeDtypeStruct((n, wv_cols), BF16),
                   jax.ShapeDtypeStruct((n // tm, 8, LANES), F32)),
        compiler_params=_params(("arbitrary",)),
        name="inproj_diff",
    )(x2d, g, wq, qc, wk, pf, wv, vc, iqk, iall)


def _gla_kernel(q_ref, k_ref, la_ref, v_ref, sg_ref, gn_ref, o_ref, st_ref, a_ref, *, tg):
    @pl.when(pl.program_id(1) == 0)
    def _():
        st_ref[...] = jnp.zeros(st_ref.shape, F32)

    la = la_ref[...]
    la_hi, la_lo = _split_bf16(la)
    row = lax.broadcasted_iota(jnp.int32, (tg, tg), 0)
    col = lax.broadcasted_iota(jnp.int32, (tg, tg), 1)
    causal = col <= row

    def prefix(sel):
        sel = jnp.where(sel, 1.0, 0.0).astype(BF16)
        return _dot(sel, la_hi) + _dot(sel, la_lo)

    b = prefix(causal)
    btot = b[tg - 1:tg, :]
    q = q_ref[...]
    k = k_ref[...]
    qe = (q * jnp.exp(b)).astype(BF16)
    kd = (k * jnp.exp(btot - b)).astype(BF16)

    safe = jnp.min(btot) >= -GLA_SAFE_LOG_DECAY

    @pl.when(safe)
    def _():
        ke = (k * jnp.exp(-b)).astype(BF16)
        for hd in range(GLA_HEADS):
            sl = slice(hd * GLA_DK, (hd + 1) * GLA_DK)
            a = _dot_nt(qe[:, sl], ke[:, sl])
            a_ref[hd] = jnp.where(causal, a, 0.0).astype(BF16)

    @pl.when(jnp.logical_not(safe))
    def _():
        qb = q.astype(BF16)
        kb = k.astype(BF16)
        acc = [jnp.where(row == col, _dot_nt(qb[:, hd * GLA_DK:(hd + 1) * GLA_DK],
                                             kb[:, hd * GLA_DK:(hd + 1) * GLA_DK]), 0.0)
               for hd in range(GLA_HEADS)]
        rowv = lax.broadcasted_iota(jnp.int32, (tg, 1), 0)
        half = tg // 2
        while half >= 1:
            shift = int(math.log2(2 * half))
            mid = ((row >> shift) << shift) + (half - 1)
            r = prefix(col <= mid)
            second = (rowv & (2 * half - 1)) >= half
            fac = jnp.exp(jnp.where(second, b - r, r - b))
            ql = jnp.where(second, q * fac, 0.0).astype(BF16)
            kl = jnp.where(second, 0.0, k * fac).astype(BF16)
            same = (row >> shift) == (col >> shift)
            for hd in range(GLA_HEADS):
                sl = slice(hd * GLA_DK, (hd + 1) * GLA_DK)
                acc[hd] = acc[hd] + jnp.where(same, _dot_nt(ql[:, sl], kl[:, sl]), 0.0)
            half //= 2
        for hd in range(GLA_HEADS):
            a_ref[hd] = acc[hd].astype(BF16)

    v = v_ref[...]
    gn = gn_ref[...]
    decay = jnp.exp(btot)
    for hd in range(GLA_HEADS):
        sl = slice(hd * GLA_DK, (hd + 1) * GLA_DK)
        vs = slice(hd * GLA_DV, (hd + 1) * GLA_DV)
        st = st_ref[hd]
        o = _dot(a_ref[hd], v[:, vs]) + _dot_nt(qe[:, sl], st.astype(BF16))
        st_ref[hd] = st * decay[:, sl] + _dot_tn(v[:, vs], kd[:, sl])
        o_ref[:, vs] = (_rms(o, gn) * sg_ref[:, vs].astype(F32)).astype(BF16)


def _gla(gq, gk, la, gv, sg, gn, *, batch, seq):
    n = gq.shape[0]
    tg = min(GLA_TILE, seq)
    tps = seq // tg
    row = lambda w: pl.BlockSpec((tg, w), lambda b, j: (b * tps + j, 0))
    return pl.pallas_call(
        functools.partial(_gla_kernel, tg=tg),
        grid=(batch, tps),
        in_specs=[row(GLA_QK), row(GLA_QK), row(GLA_QK), row(GLA_V), row(GLA_V),
                  pl.BlockSpec(gn.shape, lambda b, j: (0, 0))],
        out_specs=row(GLA_V),
        out_shape=jax.ShapeDtypeStruct((n, GLA_V), BF16),
        scratch_shapes=[pltpu.VMEM((GLA_HEADS, GLA_DV, GLA_DK), F32),
                        pltpu.VMEM((GLA_HEADS, tg, tg), BF16)],
        compiler_params=_params(("arbitrary", "arbitrary")),
        name="gla",
    )(gq, gk, la, gv, sg, gn)


def _attn_kernel(qi_ref, kj_ref, kind_ref, first_ref, q_ref, kt_ref, v_ref, lam_ref, gn_ref, o_ref,
                 m_ref, acc_ref, *, lam_init, tq, tk, steps):
    idx = pl.program_id(0) * steps + pl.program_id(1)
    kind = kind_ref[idx]
    ar = min(ATTN_ROWS, tq)
    n_diag = tq // tk

    @pl.when(first_ref[idx] == 1)
    def _():
        m_ref[...] = jnp.full(m_ref.shape, -jnp.inf, F32)
        acc_ref[...] = jnp.zeros(acc_ref.shape, F32)

    def chain(rb, mp, c_end, rel):
        rows = slice(rb * ar, (rb + 1) * ar)
        s = _dot(q_ref[rows, mp * LANES:(mp + 1) * LANES], kt_ref[mp * LANES:(mp + 1) * LANES, 0:c_end])
        if rel is not None:
            row = lax.broadcasted_iota(jnp.int32, (ar, c_end), 0) + rel
            col = lax.broadcasted_iota(jnp.int32, (ar, c_end), 1)
            s = jnp.where(col <= row, s, -jnp.inf)
        m_old = m_ref[mp, rows, :]
        m_new = jnp.maximum(m_old, jnp.max(s, axis=1, keepdims=True))
        p = jnp.exp(s - m_new).astype(BF16)
        acc_ref[mp, rows, :] = (jnp.exp(m_old - m_new) * acc_ref[mp, rows, :]
                                + _dot(p, v_ref[0:c_end, :]))
        m_ref[mp, rows, :] = m_new

    def step(d):
        for rb in range(tq // ar):
            rel = None if d is None else rb * ar - d * tk
            if rel is not None and rel + ar <= 0:
                continue
            if rel is None or rel >= tk - 1:
                c_end, rel = tk, None
            else:
                c_end = min(tk, -(-(rel + ar) // MXU_COLS) * MXU_COLS)
            for mp in range(2):
                chain(rb, mp, c_end, rel)

    @pl.when(kind == -1)
    def _():
        step(None)

    for d in range(n_diag):
        @pl.when(kind == d)
        def _(d=d):
            step(d)

    @pl.when(kind == n_diag - 1)
    def _():
        lp = lam_ref[...]
        lam = (jnp.exp(jnp.sum(lp[0:1] * lp[1:2], axis=1, keepdims=True))
               - jnp.exp(jnp.sum(lp[2:3] * lp[3:4], axis=1, keepdims=True)) + lam_init)
        a1 = acc_ref[0]
        a2 = acc_ref[1]
        o = (a1[:, :DIFF_DV] / a1[:, DIFF_DV:DIFF_DV + 1]
             - lam * (a2[:, :DIFF_DV] / a2[:, DIFF_DV:DIFF_DV + 1]))
        o_ref[...] = (_rms(o, gn_ref[...]) * (1.0 - lam_init)).astype(BF16)


def _attn_tiles(seq):
    tk = min(ATTN_K_TILE, seq)
    tq = min(ATTN_Q_TILE, seq)
    return tq, tk


def _attn_steps(stats, *, batch, seq):
    tq, tk = _attn_tiles(seq)
    nq, nk, n_diag = seq // tq, seq // tk, tq // tk
    tm = min(TOKEN_TILE, seq)
    pairs = [(i, j) for i in range(nq) for j in range((i + 1) * n_diag)]
    qi_s = np.array([p[0] for p in pairs], np.int32)
    kj_s = np.array([p[1] for p in pairs], np.int32)
    kind_s = np.array([max(j - i * n_diag, -1) for i, j in pairs], np.int32)
    steps = len(pairs)

    nhm = 2 * DIFF_HEADS
    per_tile = stats.reshape(batch, seq // tm, 8, LANES)
    q2 = jnp.max(per_tile[:, :, 0, :nhm].reshape(batch, nq, tq // tm, nhm), axis=2)
    k2 = jnp.max(per_tile[:, :, 1, :nhm].reshape(batch, nk, tk // tm, nhm), axis=2)
    dmin = jnp.min(per_tile[:, :, 2, :nhm].reshape(batch, nq, tq // tm, nhm), axis=2)
    slopes = np.repeat([2.0 ** (-8.0 * (hd + 1.0) / DIFF_HEADS) for hd in range(DIFF_HEADS)], 2)
    last_key = ((kj_s + 1) * tk - 1).astype(np.float32)
    bound = (jnp.sqrt(q2[:, qi_s, :] * k2[:, kj_s, :]) * (1.0 + 1e-3)
             + jnp.asarray(slopes, F32)[None, None, :] * jnp.asarray(last_key)[None, :, None])
    negligible = bound < dmin[:, qi_s, :] - ATTN_SKIP_MARGIN
    negligible = jnp.all(negligible.reshape(batch, steps, DIFF_HEADS, 2), axis=3)
    active = jnp.logical_or(jnp.asarray(kind_s >= 0)[None, :, None], jnp.logical_not(negligible))
    active = active.transpose(0, 2, 1).reshape(batch * DIFF_HEADS, steps)

    pos = jnp.arange(steps, dtype=jnp.int32)
    dest = jnp.cumsum(active.astype(jnp.int32), axis=1) - 1
    n_act = dest[:, -1:] + 1
    sel = jnp.logical_and(active[:, :, None], dest[:, :, None] == pos[None, None, :])
    pick = lambda tab: jnp.sum(jnp.where(sel, jnp.asarray(tab)[None, :, None], 0), axis=1)
    qi, kj, kind = pick(qi_s), pick(kj_s), pick(kind_s)
    idle = pos[None, :] >= n_act
    hold = lambda a: jnp.where(idle, jnp.take_along_axis(a, n_act - 1, axis=1), a)
    qi, kj = hold(qi), hold(kj)
    kind = jnp.where(idle, -2, kind)
    prev_qi = jnp.concatenate([jnp.full_like(qi[:, :1], -1), qi[:, :-1]], axis=1)
    first = jnp.logical_and(qi != prev_qi, jnp.logical_not(idle)).astype(jnp.int32)
    flat = lambda a: a.reshape(-1).astype(jnp.int32)
    return flat(qi), flat(kj), flat(kind), flat(first), steps


def _attn(dq, dkt, dv, stats, lam_p, gn, *, batch, seq, lam_init):
    tq, tk = _attn_tiles(seq)
    qi, kj, kind, first, steps = _attn_steps(stats, batch=batch, seq=seq)
    dq3 = dq.reshape(batch, seq, dq.shape[1])
    dv3 = dv.reshape(batch, seq, dv.shape[1])
    nh = DIFF_HEADS
    at = lambda g, t: g * steps + t
    grid_spec = pltpu.PrefetchScalarGridSpec(
        num_scalar_prefetch=4,
        grid=(batch * nh, steps),
        in_specs=[
            pl.BlockSpec((None, tq, 2 * LANES), lambda g, t, qi, kj, kd, ft: (g // nh, qi[at(g, t)], g % nh)),
            pl.BlockSpec((None, 2 * LANES, tk), lambda g, t, qi, kj, kd, ft: (g // nh, g % nh, kj[at(g, t)])),
            pl.BlockSpec((None, tk, 2 * LANES), lambda g, t, qi, kj, kd, ft: (g // nh, kj[at(g, t)], g % nh)),
            pl.BlockSpec(lam_p.shape, lambda g, t, qi, kj, kd, ft: (0, 0)),
            pl.BlockSpec(gn.shape, lambda g, t, qi, kj, kd, ft: (0, 0)),
        ],
        out_specs=pl.BlockSpec((None, tq, DIFF_DV),
                               lambda g, t, qi, kj, kd, ft: (g // nh, qi[at(g, t)], g % nh)),
        scratch_shapes=[pltpu.VMEM((2, tq, 1), F32), pltpu.VMEM((2, tq, 2 * LANES), F32)],
    )
    out = pl.pallas_call(
        functools.partial(_attn_kernel, lam_init=lam_init, tq=tq, tk=tk, steps=steps),
        grid_spec=grid_spec,
        out_shape=jax.ShapeDtypeStruct((batch, seq, DIFF_V), BF16),
        compiler_params=_params(("arbitrary", "arbitrary")),
        name="diff_attn",
    )(qi, kj, kind, first, dq3, dkt, dv3, lam_p, gn)
    return out.reshape(batch * seq, DIFF_V)


def _merge_route_kernel(x_ref, oa_ref, ob_ref, oc_ref, gt_ref, woa_ref, wob_ref, woc_ref, wout_ref,
                        gf_ref, wrh_ref, wrl_ref, br_ref, x1_ref, hx_ref, cnt_ref, *, tm, d):
    i = pl.program_id(0)

    @pl.when(i == 0)
    def _():
        cnt_ref[...] = jnp.zeros(cnt_ref.shape, F32)

    ya = _dot(oa_ref[...], woa_ref[...])
    yb = _dot(ob_ref[...], wob_ref[...])
    yc = _dot(oc_ref[...], woc_ref[...])
    merged = (gt_ref[:, 0:d].astype(F32) * ya + gt_ref[:, d:2 * d].astype(F32) * yb
              + gt_ref[:, 2 * d:3 * d].astype(F32) * yc)
    x1 = x_ref[...] + _dot(merged.astype(BF16), wout_ref[...])
    x1_ref[...] = x1
    h2 = _rms(x1, gf_ref[...])
    hx_ref[:, 0:d] = h2

    h_hi, h_lo = _split_bf16(h2)
    wrh = wrh_ref[...]
    logits = _dot(h_hi, wrh) + _dot(h_lo, wrh) + _dot(h_hi, wrl_ref[...]) + br_ref[...]
    lane = lax.broadcasted_iota(jnp.int32, (tm, LANES), 1)
    ninf = -jnp.inf
    glog = jnp.where(lane < N_GROUPS, logits, ninf)
    gmax = jnp.max(glog, axis=1, keepdims=True)
    gi = jnp.min(jnp.where(glog == gmax, lane, LANES), axis=1, keepdims=True)
    gp = 1.0 / jnp.sum(jnp.exp(glog - gmax), axis=1, keepdims=True)
    eid = lane - N_GROUPS
    in_group = (eid >= 0) & (eid < N_EXPERTS) & ((eid >> EPG_SHIFT) == gi)
    w0 = jnp.where(in_group, logits, ninf)
    m1 = jnp.max(w0, axis=1, keepdims=True)
    i1 = jnp.min(jnp.where(w0 == m1, lane, LANES), axis=1, keepdims=True)
    w1 = jnp.where(lane == i1, ninf, w0)
    m2 = jnp.max(w1, axis=1, keepdims=True)
    i2 = jnp.min(jnp.where(w1 == m2, lane, LANES), axis=1, keepdims=True)
    e1 = i1 - N_GROUPS
    e2 = i2 - N_GROUPS
    tt = jnp.exp(m2 - m1)
    p1 = gp / (1.0 + tt)
    p2 = gp * tt / (1.0 + tt)
    comb = jnp.where(lane == e1, p1, 0.0) + jnp.where(lane == e2, p2, 0.0)

    lo = jnp.minimum(e1, e2) & (EXPERTS_PER_GROUP - 1)
    hi = jnp.maximum(e1, e2) & (EXPERTS_PER_GROUP - 1)
    pair = ((lo * (2 * EXPERTS_PER_GROUP - 1 - lo)) >> 1) + (hi - lo - 1)
    cls = gi * N_PAIRS + pair
    onehot = lane == cls
    row = lax.broadcasted_iota(jnp.int32, (tm, tm), 0)
    col = lax.broadcasted_iota(jnp.int32, (tm, tm), 1)
    before = jnp.where(col < row, 1.0, 0.0).astype(BF16)
    seen = _dot(before, jnp.where(onehot, 1.0, 0.0).astype(BF16)) + cnt_ref[0:1, :]
    rank = jnp.sum(jnp.where(onehot, seen, 0.0), axis=1, keepdims=True)
    cnt_ref[0:1, :] = cnt_ref[0:1, :] + jnp.sum(jnp.where(onehot, 1.0, 0.0), axis=0, keepdims=True)

    meta = comb + jnp.where(lane == META_CLS, cls.astype(F32), 0.0) \
        + jnp.where(lane == META_RANK, rank, 0.0)
    hx_ref[:, d:d + LANES] = meta


def _merge_route(x2d, oa, ob, oc, gates, woa, wob, woc, wout, gf, wrh, wrl, br, *, seq):
    n, d = x2d.shape
    tm = min(TOKEN_TILE, seq)
    row = lambda w: pl.BlockSpec((tm, w), lambda i: (i, 0))
    return pl.pallas_call(
        functools.partial(_merge_route_kernel, tm=tm, d=d),
        grid=(n // tm,),
        in_specs=[row(d), row(GLA_V), row(DIFF_V), row(CONV_WIDTH), row(N_BRANCH * d),
                  _const_spec(woa.shape), _const_spec(wob.shape), _const_spec(woc.shape),
                  _const_spec(wout.shape), _const_spec(gf.shape), _const_spec(wrh.shape),
                  _const_spec(wrl.shape), _const_spec(br.shape)],
        out_specs=(row(d), row(d + LANES), pl.BlockSpec((8, LANES), lambda i: (0, 0))),
        out_shape=(jax.ShapeDtypeStruct((n, d), F32), jax.ShapeDtypeStruct((n, d + LANES), F32),
                   jax.ShapeDtypeStruct((8, LANES), F32)),
        compiler_params=_params(("arbitrary",)),
        name="merge_route",
    )(x2d, oa, ob, oc, gates, woa, wob, woc, wout, gf, wrh, wrl, br)


def _expert_kernel(e1_ref, e2_ref, nv_ref, src_ref, hx_ref, wgu1_ref, wgu2_ref, wdn1_ref, wdn2_ref,
                   y_ref, xbuf0_ref, xbuf1_ref, ybuf0_ref, ybuf1_ref, gsem_ref, ssem_ref,
                   *, te, d, dexp, n_tok):
    j = pl.program_id(0)
    nv = nv_ref[0]
    xbufs = (xbuf0_ref, xbuf1_ref)
    ybufs = (ybuf0_ref, ybuf1_ref)

    def gather_one(tile, r, par):
        tok = jnp.maximum(src_ref[tile * te + r], 0)
        pltpu.make_async_copy(hx_ref.at[pl.ds(tok, 1)], xbufs[par].at[pl.ds(r, 1)],
                              gsem_ref.at[par]).start()

    def scatter_one(tile, r, par):
        tok = src_ref[tile * te + r]
        dst = jnp.where(tok < 0, n_tok + par * te + r, tok)
        pltpu.make_async_copy(ybufs[par].at[pl.ds(r, 1)], y_ref.at[pl.ds(dst, 1)],
                              ssem_ref.at[par]).start()

    def wait_gather(par):
        pltpu.make_async_copy(hx_ref.at[pl.ds(0, te)], xbufs[par], gsem_ref.at[par]).wait()

    def wait_scatter(par):
        pltpu.make_async_copy(ybufs[par], y_ref.at[pl.ds(0, te)], ssem_ref.at[par]).wait()

    def compute(par):
        xb = xbufs[par]
        xs = xb[:, 0:d].astype(BF16)
        meta = xb[:, d:d + LANES]
        lane = lax.broadcasted_iota(jnp.int32, (te, LANES), 1)
        out = jnp.zeros((te, d), F32)
        for e_ref, wgu_ref, wdn_ref in ((e1_ref, wgu1_ref, wdn1_ref), (e2_ref, wgu2_ref, wdn2_ref)):
            gu = _dot(xs, wgu_ref[...])
            gate = gu[:, :dexp]
            act = (gate * _sigmoid(gate) * gu[:, dexp:]).astype(BF16)
            wcol = jnp.sum(jnp.where(lane == e_ref[j], meta, 0.0), axis=1, keepdims=True)
            out = out + wcol * _dot(act, wdn_ref[...])
        ybufs[par][...] = out

    def looped(fn, tile, par):
        def body(r, c):
            fn(tile, r, par)
            return c
        lax.fori_loop(0, te, body, 0)

    steady = jnp.logical_and(j >= 2, j + 1 < nv)
    for par in range(2):
        mine = (j % 2) == par

        @pl.when(jnp.logical_and(mine, steady))
        def _(par=par):
            wait_gather(par)
            wait_scatter(par)
            for r in range(te):
                gather_one(j + 1, r, 1 - par)
                scatter_one(j - 1, r, 1 - par)
            compute(par)

        @pl.when(jnp.logical_and(mine, jnp.logical_not(steady)))
        def _(par=par):
            if par == 0:
                @pl.when(j == 0)
                def _():
                    ybuf0_ref[...] = jnp.zeros((te, d), F32)
                    for k in range(2):
                        fill = pltpu.make_async_copy(ybuf0_ref, y_ref.at[pl.ds(n_tok + k * te, te)],
                                                     ssem_ref.at[0])
                        fill.start()
                        fill.wait()

                @pl.when(jnp.logical_and(j == 0, nv > 0))
                def _():
                    looped(gather_one, 0, 0)

            @pl.when(j < nv)
            def _():
                wait_gather(par)

            @pl.when(j + 1 < nv)
            def _():
                looped(gather_one, j + 1, 1 - par)

            @pl.when(jnp.logical_and(j >= 2, j - 2 < nv))
            def _():
                wait_scatter(par)

            @pl.when(jnp.logical_and(j >= 1, j - 1 < nv))
            def _():
                looped(scatter_one, j - 1, 1 - par)

            @pl.when(j < nv)
            def _():
                compute(par)


def _experts(hx, wgu, wdn, e1, e2, nvalid, src, *, te, n_tiles):
    n, dx = hx.shape
    d = dx - LANES
    dexp = wdn.shape[1]
    grid_spec = pltpu.PrefetchScalarGridSpec(
        num_scalar_prefetch=4,
        grid=(n_tiles + 2,),
        in_specs=[
            pl.BlockSpec(memory_space=pl.ANY),
            pl.BlockSpec((None, d, 2 * dexp), lambda j, e1, e2, nv, src: (e1[j], 0, 0)),
            pl.BlockSpec((None, d, 2 * dexp), lambda j, e1, e2, nv, src: (e2[j], 0, 0)),
            pl.BlockSpec((None, dexp, d), lambda j, e1, e2, nv, src: (e1[j], 0, 0)),
            pl.BlockSpec((None, dexp, d), lambda j, e1, e2, nv, src: (e2[j], 0, 0)),
        ],
        out_specs=pl.BlockSpec(memory_space=pl.ANY),
        scratch_shapes=[pltpu.VMEM((te, dx), F32), pltpu.VMEM((te, dx), F32),
                        pltpu.VMEM((te, d), F32), pltpu.VMEM((te, d), F32),
                        pltpu.SemaphoreType.DMA((2,)), pltpu.SemaphoreType.DMA((2,))],
    )
    return pl.pallas_call(
        functools.partial(_expert_kernel, te=te, d=d, dexp=dexp, n_tok=n),
        grid_spec=grid_spec,
        out_shape=jax.ShapeDtypeStruct((n + 2 * te, d), F32),
        compiler_params=_params(("arbitrary",)),
        name="experts",
    )(e1, e2, nvalid, src, hx, wgu, wgu, wdn, wdn)


_PAIR_LO = np.array([lo for lo in range(EXPERTS_PER_GROUP) for hi in range(lo + 1, EXPERTS_PER_GROUP)],
                    np.int32)
_PAIR_HI = np.array([hi for lo in range(EXPERTS_PER_GROUP) for hi in range(lo + 1, EXPERTS_PER_GROUP)],
                    np.int32)


def _dispatch_tables(hx, cnt, *, te, n_tiles):
    n = hx.shape[0]
    d = hx.shape[1] - LANES
    cls = hx[:, d + META_CLS].astype(jnp.int32)
    rank = hx[:, d + META_RANK].astype(jnp.int32)
    counts = cnt[0, :N_CLASSES].astype(jnp.int32)
    tiles_c = (counts + te - 1) // te
    tile_end = jnp.cumsum(tiles_c)
    tile_start = tile_end - tiles_c
    dest = tile_start[cls] * te + rank
    src = jnp.full((n_tiles * te,), -1, jnp.int32).at[dest].set(jnp.arange(n, dtype=jnp.int32))
    tile_ids = jnp.arange(n_tiles + 2, dtype=jnp.int32)
    tcls = jnp.minimum(jnp.sum((tile_ids[:, None] >= tile_end[None, :]).astype(jnp.int32), axis=1),
                       N_CLASSES - 1)
    grp = tcls // N_PAIRS
    pr = tcls % N_PAIRS
    e1 = grp * EXPERTS_PER_GROUP + jnp.asarray(_PAIR_LO)[pr]
    e2 = grp * EXPERTS_PER_GROUP + jnp.asarray(_PAIR_HI)[pr]
    return e1.astype(jnp.int32), e2.astype(jnp.int32), tile_end[-1:].astype(jnp.int32), src


def _ple_kernel(x1_ref, y_ref, p_ref, wpg_ref, wple_ref, gfin_ref, o_ref, *, final):
    x2 = x1_ref[...] + y_ref[...]
    gate = _sigmoid(_dot(x2.astype(BF16), wpg_ref[...]))
    x3 = x2 + gate * _dot(p_ref[...].astype(BF16), wple_ref[...])
    o_ref[...] = _rms(x3, gfin_ref[...]) if final else x3


def _ple(x1, y, p2d, wpg, wple, gfin, *, seq, final):
    n, d = x1.shape
    tm = min(TOKEN_TILE, seq)
    row = lambda w: pl.BlockSpec((tm, w), lambda i: (i, 0))
    return pl.pallas_call(
        functools.partial(_ple_kernel, final=final),
        grid=(n // tm,),
        in_specs=[row(d), row(d), row(p2d.shape[1]), _const_spec(wpg.shape), _const_spec(wple.shape),
                  _const_spec(gfin.shape)],
        out_specs=row(d),
        out_shape=jax.ShapeDtypeStruct((n, d), F32),
        compiler_params=_params(("arbitrary",)),
        name="ple",
    )(x1, y, p2d, wpg, wple, gfin)


def _pad_cols(w, width):
    return jnp.pad(w, ((0, 0), (0, width - w.shape[1])))


def _diff_weights(w_dq, w_dk, w_dv):
    d = w_dq.shape[0]
    z64 = jnp.zeros((d, LANES - DIFF_DH), w_dq.dtype)
    z128 = jnp.zeros((d, LANES), w_dq.dtype)
    wq, wk, wv = [], [], []
    for hm in range(2 * DIFF_HEADS):
        wq += [w_dq[:, hm * DIFF_DH:(hm + 1) * DIFF_DH], z64]
        wk += [w_dk[:, hm * DIFF_DH:(hm + 1) * DIFF_DH], z64]
    for hd in range(DIFF_HEADS):
        wv += [w_dv[:, hd * DIFF_DV:(hd + 1) * DIFF_DV], z128]
    cat = lambda parts: jnp.concatenate(parts, axis=1).astype(BF16)
    return cat(wq), cat(wk), cat(wv)


def _diff_constants(seq):
    slopes = [2.0 ** (-8.0 * (hd + 1.0) / DIFF_HEADS) for hd in range(DIFF_HEADS)]
    qc = np.zeros((1, 2 * DIFF_HEADS * LANES), np.float32)
    for hd in range(DIFF_HEADS):
        for mp in range(2):
            for dgt in range(N_POS_DIGITS):
                coef = slopes[hd] * float(POS_BASE ** dgt)
                mant = math.frexp(coef)[0] * 256.0
                assert mant == int(mant), "ALiBi coefficient must be exact in bf16"
                qc[0, (2 * hd + mp) * LANES + DIFF_DH + dgt] = coef
    assert seq <= POS_BASE ** N_POS_DIGITS
    pos = np.arange(seq)
    pf = np.zeros((seq, LANES), np.float32)
    for dgt in range(N_POS_DIGITS):
        pf[:, DIFF_DH + dgt] = (pos // (POS_BASE ** dgt)) % POS_BASE
    vc = np.zeros((1, DIFF_HEADS * 2 * LANES), np.float32)
    for hd in range(DIFF_HEADS):
        vc[0, hd * 2 * LANES + DIFF_DV] = 1.0
    iqk = np.zeros((2 * DIFF_HEADS * LANES, LANES), np.float32)
    iall = np.zeros((2 * DIFF_HEADS * LANES, LANES), np.float32)
    for hm in range(2 * DIFF_HEADS):
        iqk[hm * LANES:hm * LANES + DIFF_DH, hm] = 1.0
        iall[hm * LANES:(hm + 1) * LANES, hm] = 1.0
    return (jnp.asarray(qc), jnp.asarray(pf), jnp.asarray(vc), jnp.asarray(iqk, BF16),
            jnp.asarray(iall, BF16))


def kernel(x, p, g_mix, w_in, w_gla_a2, b_gla_a, g_gla_norm, w_o_gla, diff_lam, g_diff_norm, w_o_diff,
           conv_w, w_o_conv, w_out, g_ffn, w_router_group, b_router_group, w_router_expert,
           b_router_expert, w_expert_gate_up, w_expert_down, w_ple, w_ple_gate, g_final):
    batch, seq, d = x.shape
    depth = w_in.shape[0]
    n = batch * seq
    te = min(EXPERT_TILE, seq)
    n_tiles = n // te + N_CLASSES
    qc, pf, vc, iqk, iall = _diff_constants(seq)

    sizes = (GLA_QK, GLA_QK, GLA_V, GLA_V, GLA_RANK, DIFF_QK, DIFF_QK, DIFF_V,
             CONV_WIDTH, CONV_WIDTH, CONV_WIDTH, N_BRANCH * d)
    offs = np.concatenate([[0], np.cumsum(sizes)]).tolist()
    assert offs[-1] == w_in.shape[2]

    x2d = x.reshape(n, d)
    for layer in range(depth):
        wi = w_in[layer]
        seg = lambda a, b: wi[:, offs[a]:offs[b]]
        wgla = seg(0, 4).astype(BF16)
        wga = _pad_cols(seg(4, 5), LANES).astype(BF16)
        w2 = jnp.pad(w_gla_a2[layer], ((0, LANES - GLA_RANK), (0, 0))).astype(BF16)
        wq, wk, wv = _diff_weights(seg(5, 6), seg(6, 7), seg(7, 8))
        wconv = seg(8, 11).astype(BF16)
        wgates = seg(11, 12).astype(BF16)
        lam_init = 0.8 - 0.6 * math.exp(-0.3 * layer)

        gq, gk, la, gv, sg, yc, gates = _inproj_main(
            x2d, g_mix[layer][None, :], wgla, wga, w2, b_gla_a[layer][None, :], wconv,
            conv_w[layer], wgates, seq=seq)
        dq, dkt, dv, stats = _inproj_diff(x2d, g_mix[layer][None, :], wq, qc, wk, pf, wv, vc, iqk, iall,
                                          batch=batch, seq=seq)
        oa = _gla(gq, gk, la, gv, sg, g_gla_norm[layer][None, :], batch=batch, seq=seq)
        ob = _attn(dq, dkt, dv, stats, diff_lam[layer], g_diff_norm[layer][None, :],
                   batch=batch, seq=seq, lam_init=lam_init)

        wr = _pad_cols(jnp.concatenate([w_router_group[layer], w_router_expert[layer]], axis=1), LANES)
        wrh = wr.astype(BF16)
        wrl = (wr - wrh.astype(F32)).astype(BF16)
        br = _pad_cols(jnp.concatenate([b_router_group[layer], b_router_expert[layer]])[None, :], LANES)
        x1, hx, cnt = _merge_route(
            x2d, oa, ob, yc, gates, w_o_gla[layer].astype(BF16), w_o_diff[layer].astype(BF16),
            w_o_conv[layer].astype(BF16), w_out[layer].astype(BF16), g_ffn[layer][None, :],
            wrh, wrl, br, seq=seq)

        e1, e2, nvalid, src = _dispatch_tables(hx, cnt, te=te, n_tiles=n_tiles)
        y = _experts(hx, w_expert_gate_up[layer].astype(BF16), w_expert_down[layer].astype(BF16),
                     e1, e2, nvalid, src, te=te, n_tiles=n_tiles)

        x2d = _ple(x1, y, p[layer].reshape(n, p.shape[-1]), w_ple_gate[layer].astype(BF16),
                   w_ple[layer].astype(BF16), g_final[None, :], seq=seq,
                   final=(layer == depth - 1))
    return x2d.reshape(batch, seq, d)
```

```python
import functools
import math

import numpy as np
import jax
import jax.numpy as jnp
from jax import lax
from jax.experimental import pallas as pl
from jax.experimental.pallas import tpu as pltpu

F32 = jnp.float32
BF16 = jnp.bfloat16

EPS = 1e-6
GLA_HEADS = 4
GLA_DK = 64
GLA_DV = 128
GLA_RANK = 16
GLA_TAU = 16.0
GLA_QK = GLA_HEADS * GLA_DK
GLA_V = GLA_HEADS * GLA_DV
DIFF_HEADS = 4
DIFF_DH = 64
DIFF_DV = 2 * DIFF_DH
DIFF_QK = DIFF_HEADS * 2 * DIFF_DH
DIFF_V = DIFF_HEADS * DIFF_DV
CONV_WIDTH = 512
CONV_K = 3
N_GROUPS = 4
EXPERTS_PER_GROUP = 4
N_EXPERTS = N_GROUPS * EXPERTS_PER_GROUP
EPG_SHIFT = EXPERTS_PER_GROUP.bit_length() - 1
assert 1 << EPG_SHIFT == EXPERTS_PER_GROUP
N_PAIRS = EXPERTS_PER_GROUP * (EXPERTS_PER_GROUP - 1) // 2
N_CLASSES = N_GROUPS * N_PAIRS
N_BRANCH = 3

LANES = 128
POS_BASE = 128
N_POS_DIGITS = 3
GLA_SAFE_LOG_DECAY = 60.0

TOKEN_TILE = 512
WIDE_TOKEN_TILE = 1024
GLA_TILE = 256
ATTN_Q_TILE = 2048
ATTN_K_TILE = 1024
ATTN_ROWS = 128
MXU_COLS = 256
ATTN_SKIP_MARGIN = 110.0
EXPERT_TILE = 256
META_CLS = N_EXPERTS
META_RANK = N_EXPERTS + 1

VMEM_LIMIT = 56 * 1024 * 1024


def _const_spec(shape):
    nd = len(shape)
    return pl.BlockSpec(shape, lambda *_: (0,) * nd, pipeline_mode=pl.Buffered(1))


def _params(sem):
    return pltpu.CompilerParams(dimension_semantics=sem, vmem_limit_bytes=VMEM_LIMIT)


def _rms(xf, g):
    return xf * lax.rsqrt(jnp.mean(xf * xf, axis=-1, keepdims=True) + EPS) * g


def _sigmoid(v):
    return 1.0 / (1.0 + jnp.exp(-v))


def _dot(a, b):
    return jnp.dot(a, b, preferred_element_type=F32)


def _dot_nt(a, b):
    return lax.dot_general(a, b, (((1,), (1,)), ((), ())), preferred_element_type=F32)


def _dot_tn(a, b):
    return lax.dot_general(a, b, (((0,), (0,)), ((), ())), preferred_element_type=F32)


def _split_bf16(v):
    hi = v.astype(BF16)
    lo = (v - hi.astype(F32)).astype(BF16)
    return hi, lo


def _inproj_main_kernel(x_ref, g_ref, wgla_ref, wga_ref, w2_ref, ba_ref, wconv_ref, cw_ref,
                        wgates_ref, gq_ref, gk_ref, la_ref, gv_ref, sg_ref, yc_ref, gates_ref,
                        ubuf_ref, *, tiles_per_seq, tm):
    i = pl.program_id(0)
    h = _rms(x_ref[...], g_ref[...]).astype(BF16)

    z = _dot(h, wgla_ref[...])
    gq_ref[...] = z[:, :GLA_QK] * (GLA_DK ** -0.5)
    gk_ref[...] = z[:, GLA_QK:2 * GLA_QK]
    gv_ref[...] = z[:, 2 * GLA_QK:2 * GLA_QK + GLA_V].astype(BF16)
    gg = z[:, 2 * GLA_QK + GLA_V:]
    sg_ref[...] = (gg * _sigmoid(gg)).astype(BF16)

    ga = _dot(h, wga_ref[...])
    xa = _dot(ga.astype(BF16), w2_ref[...]) + ba_ref[...]
    log_sig = jnp.minimum(xa, 0.0) - jnp.log(1.0 + jnp.exp(-jnp.abs(xa)))
    la_ref[...] = log_sig * (1.0 / GLA_TAU)

    zc = _dot(h, wconv_ref[...])
    u = zc[:, 2 * CONV_WIDTH:] * zc[:, :CONV_WIDTH]

    @pl.when(i % tiles_per_seq == 0)
    def _():
        ubuf_ref[0:8, :] = jnp.zeros((8, CONV_WIDTH), F32)

    ubuf_ref[8:8 + tm, :] = u
    cw = cw_ref[...]
    y = (cw[0:1, :] * ubuf_ref[6:6 + tm, :] + cw[1:2, :] * ubuf_ref[7:7 + tm, :]
         + cw[2:3, :] * u)
    yc_ref[...] = (zc[:, CONV_WIDTH:2 * CONV_WIDTH] * y).astype(BF16)
    ubuf_ref[0:8, :] = ubuf_ref[tm:tm + 8, :]

    gates_ref[...] = _sigmoid(_dot(h, wgates_ref[...])).astype(BF16)


def _inproj_main(x2d, g, wgla, wga, w2, ba, wconv, cw, wgates, *, seq):
    n, d = x2d.shape
    tm = min(TOKEN_TILE, seq)
    row = lambda w: pl.BlockSpec((tm, w), lambda i: (i, 0))
    out_shape = (
        jax.ShapeDtypeStruct((n, GLA_QK), F32), jax.ShapeDtypeStruct((n, GLA_QK), F32),
        jax.ShapeDtypeStruct((n, GLA_QK), F32), jax.ShapeDtypeStruct((n, GLA_V), BF16),
        jax.ShapeDtypeStruct((n, GLA_V), BF16), jax.ShapeDtypeStruct((n, CONV_WIDTH), BF16),
        jax.ShapeDtypeStruct((n, N_BRANCH * d), BF16))
    return pl.pallas_call(
        functools.partial(_inproj_main_kernel, tiles_per_seq=seq // tm, tm=tm),
        grid=(n // tm,),
        in_specs=[row(d), _const_spec(g.shape), _const_spec(wgla.shape), _const_spec(wga.shape),
                  _const_spec(w2.shape), _const_spec(ba.shape), _const_spec(wconv.shape),
                  _const_spec(cw.shape), _const_spec(wgates.shape)],
        out_specs=(row(GLA_QK), row(GLA_QK), row(GLA_QK), row(GLA_V), row(GLA_V),
                   row(CONV_WIDTH), row(N_BRANCH * d)),
        out_shape=out_shape,
        scratch_shapes=[pltpu.VMEM((8 + tm, CONV_WIDTH), F32)],
        compiler_params=_params(("arbitrary",)),
        name="inproj_main",
    )(x2d, g, wgla, wga, w2, ba, wconv, cw, wgates)


def _inproj_diff_kernel(x_ref, g_ref, wq_ref, qc_ref, wk_ref, pf_ref, wv_ref, vc_ref, iqk_ref, iall_ref,
                        q_ref, kt_ref, v_ref, st_ref):
    h = _rms(x_ref[...], g_ref[...]).astype(BF16)
    qa = (_dot(h, wq_ref[...]) * (DIFF_DH ** -0.5) + qc_ref[...]).astype(BF16)
    q_ref[...] = qa
    zk = _dot(h, wk_ref[...])
    pf = pf_ref[...]
    zk = zk + jnp.concatenate([pf] * (zk.shape[1] // LANES), axis=1)
    kt_ref[...] = zk.T.astype(BF16)
    v_ref[...] = (_dot(h, wv_ref[...]) + vc_ref[...]).astype(BF16)

    qf = qa.astype(F32)
    kf = zk.astype(BF16).astype(F32)

    def block_sums(vals, ind_ref):
        hi, lo = _split_bf16(vals)
        return _dot(hi, ind_ref[...]) + _dot(lo, ind_ref[...])

    qn2 = jnp.max(block_sums(qf * qf, iqk_ref), axis=0, keepdims=True)
    kn2 = jnp.max(block_sums(kf * kf, iqk_ref), axis=0, keepdims=True)
    diag = jnp.min(block_sums(qf * kf, iall_ref), axis=0, keepdims=True)
    st_ref[...] = jnp.concatenate([qn2, kn2, diag, jnp.zeros((5, LANES), F32)], axis=0)


def _inproj_diff(x2d, g, wq, qc, wk, pf, wv, vc, iqk, iall, *, batch, seq):
    n, d = x2d.shape
    tm = min(TOKEN_TILE, seq)
    tps = seq // tm
    wq_cols, wk_cols, wv_cols = wq.shape[1], wk.shape[1], wv.shape[1]
    return pl.pallas_call(
        _inproj_diff_kernel,
        grid=(n // tm,),
        in_specs=[pl.BlockSpec((tm, d), lambda i: (i, 0)), _const_spec(g.shape),
                  _const_spec(wq.shape), _const_spec(qc.shape), _const_spec(wk.shape),
                  pl.BlockSpec((tm, LANES), lambda i: (i % tps, 0)),
                  _const_spec(wv.shape), _const_spec(vc.shape), _const_spec(iqk.shape),
                  _const_spec(iall.shape)],
        out_specs=(pl.BlockSpec((tm, wq_cols), lambda i: (i, 0)),
                   pl.BlockSpec((None, wk_cols, tm), lambda i: (i // tps, 0, i % tps)),
                   pl.BlockSpec((tm, wv_cols), lambda i: (i, 0)),
                   pl.BlockSpec((None, 8, LANES), lambda i: (i, 0, 0))),
        out_shape=(jax.ShapeDtypeStruct((n, wq_cols), BF16),
                   jax.ShapeDtypeStruct((batch, wk_cols, seq), BF16),
                   jax.ShapeDtypeStruct((n, wv_cols), BF16),
                   jax.ShapeDtypeStruct((n // tm, 8, LANES), F32)),
        compiler_params=_params(("arbitrary",)),
        name="inproj_diff",
    )(x2d, g, wq, qc, wk, pf, wv, vc, iqk, iall)


def _gla_kernel(q_ref, k_ref, la_ref, v_ref, sg_ref, gn_ref, o_ref, st_ref, a_ref, *, tg):
    @pl.when(pl.program_id(1) == 0)
    def _():
        st_ref[...] = jnp.zeros(st_ref.shape, F32)

    la = la_ref[...]
    la_hi, la_lo = _split_bf16(la)
    row = lax.broadcasted_iota(jnp.int32, (tg, tg), 0)
    col = lax.broadcasted_iota(jnp.int32, (tg, tg), 1)
    causal = col <= row

    def prefix(sel):
        sel = jnp.where(sel, 1.0, 0.0).astype(BF16)
        return _dot(sel, la_hi) + _dot(sel, la_lo)

    b = prefix(causal)
    btot = b[tg - 1:tg, :]
    q = q_ref[...]
    k = k_ref[...]
    qe = (q * jnp.exp(b)).astype(BF16)
    kd = (k * jnp.exp(btot - b)).astype(BF16)

    safe = jnp.min(btot) >= -GLA_SAFE_LOG_DECAY

    @pl.when(safe)
    def _():
        ke = (k * jnp.exp(-b)).astype(BF16)
        for hd in range(GLA_HEADS):
            sl = slice(hd * GLA_DK, (hd + 1) * GLA_DK)
            a = _dot_nt(qe[:, sl], ke[:, sl])
            a_ref[hd] = jnp.where(causal, a, 0.0).astype(BF16)

    @pl.when(jnp.logical_not(safe))
    def _():
        qb = q.astype(BF16)
        kb = k.astype(BF16)
        acc = [jnp.where(row == col, _dot_nt(qb[:, hd * GLA_DK:(hd + 1) * GLA_DK],
                                             kb[:, hd * GLA_DK:(hd + 1) * GLA_DK]), 0.0)
               for hd in range(GLA_HEADS)]
        rowv = lax.broadcasted_iota(jnp.int32, (tg, 1), 0)
        half = tg // 2
        while half >= 1:
            shift = int(math.log2(2 * half))
            mid = ((row >> shift) << shift) + (half - 1)
            r = prefix(col <= mid)
            second = (rowv & (2 * half - 1)) >= half
            fac = jnp.exp(jnp.where(second, b - r, r - b))
            ql = jnp.where(second, q * fac, 0.0).astype(BF16)
            kl = jnp.where(second, 0.0, k * fac).astype(BF16)
            same = (row >> shift) == (col >> shift)
            for hd in range(GLA_HEADS):
                sl = slice(hd * GLA_DK, (hd + 1) * GLA_DK)
                acc[hd] = acc[hd] + jnp.where(same, _dot_nt(ql[:, sl], kl[:, sl]), 0.0)
            half //= 2
        for hd in range(GLA_HEADS):
            a_ref[hd] = acc[hd].astype(BF16)

    v = v_ref[...]
    gn = gn_ref[...]
    decay = jnp.exp(btot)
    for hd in range(GLA_HEADS):
        sl = slice(hd * GLA_DK, (hd + 1) * GLA_DK)
        vs = slice(hd * GLA_DV, (hd + 1) * GLA_DV)
        st = st_ref[hd]
        o = _dot(a_ref[hd], v[:, vs]) + _dot_nt(qe[:, sl], st.astype(BF16))
        st_ref[hd] = st * decay[:, sl] + _dot_tn(v[:, vs], kd[:, sl])
        o_ref[:, vs] = (_rms(o, gn) * sg_ref[:, vs].astype(F32)).astype(BF16)


def _gla(gq, gk, la, gv, sg, gn, *, batch, seq):
    n = gq.shape[0]
    tg = min(GLA_TILE, seq)
    tps = seq // tg
    row = lambda w: pl.BlockSpec((tg, w), lambda b, j: (b * tps + j, 0))
    return pl.pallas_call(
        functools.partial(_gla_kernel, tg=tg),
        grid=(batch, tps),
        in_specs=[row(GLA_QK), row(GLA_QK), row(GLA_QK), row(GLA_V), row(GLA_V),
                  pl.BlockSpec(gn.shape, lambda b, j: (0, 0))],
        out_specs=row(GLA_V),
        out_shape=jax.ShapeDtypeStruct((n, GLA_V), BF16),
        scratch_shapes=[pltpu.VMEM((GLA_HEADS, GLA_DV, GLA_DK), F32),
                        pltpu.VMEM((GLA_HEADS, tg, tg), BF16)],
        compiler_params=_params(("arbitrary", "arbitrary")),
        name="gla",
    )(gq, gk, la, gv, sg, gn)


def _attn_kernel(qi_ref, kj_ref, kind_ref, first_ref, q_ref, kt_ref, v_ref, *rest, lam_init, tq, tk, steps):
    n_diag = tq // tk
    if n_diag == 2:
        kt2_ref, v2_ref, lam_ref, gn_ref, o_ref, m_ref, acc_ref = rest
    else:
        lam_ref, gn_ref, o_ref, m_ref, acc_ref = rest
    idx = pl.program_id(0) * steps + pl.program_id(1)
    kind = kind_ref[idx]
    ar = min(ATTN_ROWS, tq)

    @pl.when(first_ref[idx] == 1)
    def _():
        m_ref[...] = jnp.full(m_ref.shape, -jnp.inf, F32)
        acc_ref[...] = jnp.zeros(acc_ref.shape, F32)

    def chain(rb, mp, pieces):
        rows = slice(rb * ar, (rb + 1) * ar)
        q = q_ref[rows, mp * LANES:(mp + 1) * LANES]
        scores = []
        for ktr, _, c_end, rel in pieces:
            s = _dot(q, ktr[mp * LANES:(mp + 1) * LANES, 0:c_end])
            if rel is not None:
                row = lax.broadcasted_iota(jnp.int32, (ar, c_end), 0) + rel
                col = lax.broadcasted_iota(jnp.int32, (ar, c_end), 1)
                s = jnp.where(col <= row, s, -jnp.inf)
            scores.append(s)
        m_old = m_ref[mp, rows, :]
        m_new = m_old
        for s in scores:
            m_new = jnp.maximum(m_new, jnp.max(s, axis=1, keepdims=True))
        acc = jnp.exp(m_old - m_new) * acc_ref[mp, rows, :]
        for s, (_, vr, c_end, _) in zip(scores, pieces):
            acc = acc + _dot(jnp.exp((s - m_new).astype(BF16)), vr[0:c_end, :])
        acc_ref[mp, rows, :] = acc
        m_ref[mp, rows, :] = m_new

    def step(diagonal):
        tiles = [(kt_ref, v_ref), (kt2_ref, v2_ref)][:n_diag] if diagonal else [(kt_ref, v_ref)]
        for rb in range(tq // ar):
            pieces = []
            for d, (ktr, vr) in enumerate(tiles):
                rel = rb * ar - d * tk if diagonal else tk
                if rel + ar <= 0:
                    continue
                if rel >= tk - 1:
                    pieces.append((ktr, vr, tk, None))
                else:
                    pieces.append((ktr, vr, min(tk, -(-(rel + ar) // MXU_COLS) * MXU_COLS), rel))
            for mp in range(2):
                chain(rb, mp, pieces)

    @pl.when(kind == -1)
    def _():
        step(False)

    @pl.when(kind == 0)
    def _():
        step(True)
        lp = lam_ref[...]
        lam = (jnp.exp(jnp.sum(lp[0:1] * lp[1:2], axis=1, keepdims=True))
               - jnp.exp(jnp.sum(lp[2:3] * lp[3:4], axis=1, keepdims=True)) + lam_init)
        a1 = acc_ref[0]
        a2 = acc_ref[1]
        o = (a1[:, :DIFF_DV] / a1[:, DIFF_DV:DIFF_DV + 1]
             - lam * (a2[:, :DIFF_DV] / a2[:, DIFF_DV:DIFF_DV + 1]))
        o_ref[...] = (_rms(o, gn_ref[...]) * (1.0 - lam_init)).astype(BF16)


def _attn_tiles(seq):
    tk = min(ATTN_K_TILE, seq)
    tq = min(ATTN_Q_TILE, seq)
    return tq, tk


def _attn_steps(stats, *, batch, seq):
    tq, tk = _attn_tiles(seq)
    nq, nk, n_diag = seq // tq, seq // tk, tq // tk
    tm = min(TOKEN_TILE, seq)
    assert n_diag in (1, 2)
    pairs = [(i, j) for i in range(nq) for j in range(i * n_diag + 1)]
    qi_s = np.array([p[0] for p in pairs], np.int32)
    kj_s = np.array([p[1] for p in pairs], np.int32)
    kind_s = np.array([0 if j == i * n_diag else -1 for i, j in pairs], np.int32)
    steps = len(pairs)

    nhm = 2 * DIFF_HEADS
    per_tile = stats.reshape(batch, seq // tm, 8, LANES)
    q2 = jnp.max(per_tile[:, :, 0, :nhm].reshape(batch, nq, tq // tm, nhm), axis=2)
    k2 = jnp.max(per_tile[:, :, 1, :nhm].reshape(batch, nk, tk // tm, nhm), axis=2)
    dmin = jnp.min(per_tile[:, :, 2, :nhm].reshape(batch, nq, tq // tm, nhm), axis=2)
    slopes = np.repeat([2.0 ** (-8.0 * (hd + 1.0) / DIFF_HEADS) for hd in range(DIFF_HEADS)], 2)
    last_key = ((kj_s + 1) * tk - 1).astype(np.float32)
    bound = (jnp.sqrt(q2[:, qi_s, :] * k2[:, kj_s, :]) * (1.0 + 1e-3)
             + jnp.asarray(slopes, F32)[None, None, :] * jnp.asarray(last_key)[None, :, None])
    negligible = bound < dmin[:, qi_s, :] - ATTN_SKIP_MARGIN
    negligible = jnp.all(negligible.reshape(batch, steps, DIFF_HEADS, 2), axis=3)
    active = jnp.logical_or(jnp.asarray(kind_s >= 0)[None, :, None], jnp.logical_not(negligible))
    active = active.transpose(0, 2, 1).reshape(batch * DIFF_HEADS, steps)

    pos = jnp.arange(steps, dtype=jnp.int32)
    dest = jnp.cumsum(active.astype(jnp.int32), axis=1) - 1
    n_act = dest[:, -1:] + 1
    sel = jnp.logical_and(active[:, :, None], dest[:, :, None] == pos[None, None, :])
    pick = lambda tab: jnp.sum(jnp.where(sel, jnp.asarray(tab)[None, :, None], 0), axis=1)
    qi, kj, kind = pick(qi_s), pick(kj_s), pick(kind_s)
    idle = pos[None, :] >= n_act
    hold = lambda a: jnp.where(idle, jnp.take_along_axis(a, n_act - 1, axis=1), a)
    qi, kj = hold(qi), hold(kj)
    kind = jnp.where(idle, -2, kind)
    prev_qi = jnp.concatenate([jnp.full_like(qi[:, :1], -1), qi[:, :-1]], axis=1)
    first = jnp.logical_and(qi != prev_qi, jnp.logical_not(idle)).astype(jnp.int32)
    flat = lambda a: a.reshape(-1).astype(jnp.int32)
    return flat(qi), flat(kj), flat(kind), flat(first), steps


def _attn(dq, dkt, dv, stats, lam_p, gn, *, batch, seq, lam_init):
    tq, tk = _attn_tiles(seq)
    qi, kj, kind, first, steps = _attn_steps(stats, batch=batch, seq=seq)
    dq3 = dq.reshape(batch, seq, dq.shape[1])
    dv3 = dv.reshape(batch, seq, dv.shape[1])
    nh = DIFF_HEADS
    n_diag = tq // tk
    at = lambda g, t: g * steps + t
    last = lambda qi, g, t: n_diag * qi[at(g, t)] + n_diag - 1
    second = [
        pl.BlockSpec((None, 2 * LANES, tk), lambda g, t, qi, kj, kd, ft: (g // nh, g % nh, last(qi, g, t))),
        pl.BlockSpec((None, tk, 2 * LANES), lambda g, t, qi, kj, kd, ft: (g // nh, last(qi, g, t), g % nh)),
    ] if n_diag == 2 else []
    grid_spec = pltpu.PrefetchScalarGridSpec(
        num_scalar_prefetch=4,
        grid=(batch * nh, steps),
        in_specs=[
            pl.BlockSpec((None, tq, 2 * LANES), lambda g, t, qi, kj, kd, ft: (g // nh, qi[at(g, t)], g % nh)),
            pl.BlockSpec((None, 2 * LANES, tk), lambda g, t, qi, kj, kd, ft: (g // nh, g % nh, kj[at(g, t)])),
            pl.BlockSpec((None, tk, 2 * LANES), lambda g, t, qi, kj, kd, ft: (g // nh, kj[at(g, t)], g % nh)),
        ] + second + [
            pl.BlockSpec(lam_p.shape, lambda g, t, qi, kj, kd, ft: (0, 0)),
            pl.BlockSpec(gn.shape, lambda g, t, qi, kj, kd, ft: (0, 0)),
        ],
        out_specs=pl.BlockSpec((None, tq, DIFF_DV),
                               lambda g, t, qi, kj, kd, ft: (g // nh, qi[at(g, t)], g % nh)),
        scratch_shapes=[pltpu.VMEM((2, tq, 1), F32), pltpu.VMEM((2, tq, 2 * LANES), F32)],
    )
    out = pl.pallas_call(
        functools.partial(_attn_kernel, lam_init=lam_init, tq=tq, tk=tk, steps=steps),
        grid_spec=grid_spec,
        out_shape=jax.ShapeDtypeStruct((batch, seq, DIFF_V), BF16),
        compiler_params=_params(("arbitrary", "arbitrary")),
        name="diff_attn",
    )(qi, kj, kind, first, dq3, dkt, dv3, *([dkt, dv3] if n_diag == 2 else []), lam_p, gn)
    return out.reshape(batch * seq, DIFF_V)


def _merge_route_kernel(x_ref, oa_ref, ob_ref, oc_ref, gt_ref, woa_ref, wob_ref, woc_ref, wout_ref,
                        gf_ref, wrh_ref, wrl_ref, br_ref, x1_ref, hx_ref, cnt_ref, *, tm, d):
    i = pl.program_id(0)

    @pl.when(i == 0)
    def _():
        cnt_ref[...] = jnp.zeros(cnt_ref.shape, F32)

    ya = _dot(oa_ref[...], woa_ref[...])
    yb = _dot(ob_ref[...], wob_ref[...])
    yc = _dot(oc_ref[...], woc_ref[...])
    merged = (gt_ref[:, 0:d].astype(F32) * ya + gt_ref[:, d:2 * d].astype(F32) * yb
              + gt_ref[:, 2 * d:3 * d].astype(F32) * yc)
    x1 = x_ref[...] + _dot(merged.astype(BF16), wout_ref[...])
    x1_ref[...] = x1
    h2 = _rms(x1, gf_ref[...])
    hx_ref[:, 0:d] = h2

    h_hi, h_lo = _split_bf16(h2)
    wrh = wrh_ref[...]
    logits = _dot(h_hi, wrh) + _dot(h_lo, wrh) + _dot(h_hi, wrl_ref[...]) + br_ref[...]
    lane = lax.broadcasted_iota(jnp.int32, (tm, LANES), 1)
    ninf = -jnp.inf
    glog = jnp.where(lane < N_GROUPS, logits, ninf)
    gmax = jnp.max(glog, axis=1, keepdims=True)
    gi = jnp.min(jnp.where(glog == gmax, lane, LANES), axis=1, keepdims=True)
    gp = 1.0 / jnp.sum(jnp.exp(glog - gmax), axis=1, keepdims=True)
    eid = lane - N_GROUPS
    in_group = (eid >= 0) & (eid < N_EXPERTS) & ((eid >> EPG_SHIFT) == gi)
    w0 = jnp.where(in_group, logits, ninf)
    m1 = jnp.max(w0, axis=1, keepdims=True)
    i1 = jnp.min(jnp.where(w0 == m1, lane, LANES), axis=1, keepdims=True)
    w1 = jnp.where(lane == i1, ninf, w0)
    m2 = jnp.max(w1, axis=1, keepdims=True)
    i2 = jnp.min(jnp.where(w1 == m2, lane, LANES), axis=1, keepdims=True)
    e1 = i1 - N_GROUPS
    e2 = i2 - N_GROUPS
    tt = jnp.exp(m2 - m1)
    p1 = gp / (1.0 + tt)
    p2 = gp * tt / (1.0 + tt)
    comb = jnp.where(lane == e1, p1, 0.0) + jnp.where(lane == e2, p2, 0.0)

    lo = jnp.minimum(e1, e2) & (EXPERTS_PER_GROUP - 1)
    hi = jnp.maximum(e1, e2) & (EXPERTS_PER_GROUP - 1)
    pair = ((lo * (2 * EXPERTS_PER_GROUP - 1 - lo)) >> 1) + (hi - lo - 1)
    cls = gi * N_PAIRS + pair
    onehot = lane == cls
    row = lax.broadcasted_iota(jnp.int32, (tm, tm), 0)
    col = lax.broadcasted_iota(jnp.int32, (tm, tm), 1)
    before = jnp.where(col < row, 1.0, 0.0).astype(BF16)
    seen = _dot(before, jnp.where(onehot, 1.0, 0.0).astype(BF16)) + cnt_ref[0:1, :]
    rank = jnp.sum(jnp.where(onehot, seen, 0.0), axis=1, keepdims=True)
    cnt_ref[0:1, :] = cnt_ref[0:1, :] + jnp.sum(jnp.where(onehot, 1.0, 0.0), axis=0, keepdims=True)

    meta = comb + jnp.where(lane == META_CLS, cls.astype(F32), 0.0) \
        + jnp.where(lane == META_RANK, rank, 0.0)
    hx_ref[:, d:d + LANES] = meta


def _merge_route(x2d, oa, ob, oc, gates, woa, wob, woc, wout, gf, wrh, wrl, br, *, seq):
    n, d = x2d.shape
    tm = min(TOKEN_TILE, seq)
    row = lambda w: pl.BlockSpec((tm, w), lambda i: (i, 0))
    return pl.pallas_call(
        functools.partial(_merge_route_kernel, tm=tm, d=d),
        grid=(n // tm,),
        in_specs=[row(d), row(GLA_V), row(DIFF_V), row(CONV_WIDTH), row(N_BRANCH * d),
                  _const_spec(woa.shape), _const_spec(wob.shape), _const_spec(woc.shape),
                  _const_spec(wout.shape), _const_spec(gf.shape), _const_spec(wrh.shape),
                  _const_spec(wrl.shape), _const_spec(br.shape)],
        out_specs=(row(d), row(d + LANES), pl.BlockSpec((8, LANES), lambda i: (0, 0))),
        out_shape=(jax.ShapeDtypeStruct((n, d), F32), jax.ShapeDtypeStruct((n, d + LANES), F32),
                   jax.ShapeDtypeStruct((8, LANES), F32)),
        compiler_params=_params(("arbitrary",)),
        name="merge_route",
    )(x2d, oa, ob, oc, gates, woa, wob, woc, wout, gf, wrh, wrl, br)


def _expert_kernel(e1_ref, e2_ref, nv_ref, src_ref, dst_ref, hx_ref, wgu1_ref, wgu2_ref, wdn1_ref, wdn2_ref,
                   y_ref, xbuf0_ref, xbuf1_ref, ybuf0_ref, ybuf1_ref, gsem_ref, ssem_ref,
                   *, te, d, dexp, n_tok):
    j = pl.program_id(0)
    nv = nv_ref[0]
    xbufs = (xbuf0_ref, xbuf1_ref)
    ybufs = (ybuf0_ref, ybuf1_ref)

    def gather_one(tile, r, par):
        pltpu.make_async_copy(hx_ref.at[pl.ds(src_ref[tile * te + r], 1)], xbufs[par].at[pl.ds(r, 1)],
                              gsem_ref.at[par]).start()

    def scatter_one(tile, r, par):
        pltpu.make_async_copy(ybufs[par].at[pl.ds(r, 1)], y_ref.at[pl.ds(dst_ref[tile * te + r], 1)],
                              ssem_ref.at[par]).start()

    def wait_gather(par):
        pltpu.make_async_copy(hx_ref.at[pl.ds(0, te)], xbufs[par], gsem_ref.at[par]).wait()

    def wait_scatter(par):
        pltpu.make_async_copy(ybufs[par], y_ref.at[pl.ds(0, te)], ssem_ref.at[par]).wait()

    def compute(par):
        xb = xbufs[par]
        xs = xb[:, 0:d].astype(BF16)
        meta = xb[:, d:d + LANES]
        lane = lax.broadcasted_iota(jnp.int32, (te, LANES), 1)
        out = jnp.zeros((te, d), F32)
        for e_ref, wgu_ref, wdn_ref in ((e1_ref, wgu1_ref, wdn1_ref), (e2_ref, wgu2_ref, wdn2_ref)):
            gu = _dot(xs, wgu_ref[...])
            gate = gu[:, :dexp]
            act = (gate * _sigmoid(gate) * gu[:, dexp:]).astype(BF16)
            wcol = jnp.sum(jnp.where(lane == e_ref[j], meta, 0.0), axis=1, keepdims=True)
            out = out + wcol * _dot(act, wdn_ref[...])
        ybufs[par][...] = out

    def looped(fn, tile, par):
        def body(r, c):
            fn(tile, r, par)
            return c
        lax.fori_loop(0, te, body, 0)

    steady = jnp.logical_and(j >= 2, j + 1 < nv)
    for par in range(2):
        mine = (j % 2) == par

        @pl.when(jnp.logical_and(mine, steady))
        def _(par=par):
            wait_gather(par)
            wait_scatter(par)
            for r in range(te):
                gather_one(j + 1, r, 1 - par)
                scatter_one(j - 1, r, 1 - par)
            compute(par)

        @pl.when(jnp.logical_and(mine, jnp.logical_not(steady)))
        def _(par=par):
            if par == 0:
                @pl.when(j == 0)
                def _():
                    ybuf0_ref[...] = jnp.zeros((te, d), F32)
                    for k in range(2):
                        fill = pltpu.make_async_copy(ybuf0_ref, y_ref.at[pl.ds(n_tok + k * te, te)],
                                                     ssem_ref.at[0])
                        fill.start()
                        fill.wait()

                @pl.when(jnp.logical_and(j == 0, nv > 0))
                def _():
                    looped(gather_one, 0, 0)

            @pl.when(j < nv)
            def _():
                wait_gather(par)

            @pl.when(j + 1 < nv)
            def _():
                looped(gather_one, j + 1, 1 - par)

            @pl.when(jnp.logical_and(j >= 2, j - 2 < nv))
            def _():
                wait_scatter(par)

            @pl.when(jnp.logical_and(j >= 1, j - 1 < nv))
            def _():
                looped(scatter_one, j - 1, 1 - par)

            @pl.when(j < nv)
            def _():
                compute(par)


def _experts(hx, wgu, wdn, e1, e2, nvalid, src, dst, *, te, n_tiles):
    n, dx = hx.shape
    d = dx - LANES
    dexp = wdn.shape[1]
    grid_spec = pltpu.PrefetchScalarGridSpec(
        num_scalar_prefetch=5,
        grid=(n_tiles + 2,),
        in_specs=[
            pl.BlockSpec(memory_space=pl.ANY),
            pl.BlockSpec((None, d, 2 * dexp), lambda j, e1, e2, nv, src, dst: (e1[j], 0, 0)),
            pl.BlockSpec((None, d, 2 * dexp), lambda j, e1, e2, nv, src, dst: (e2[j], 0, 0)),
            pl.BlockSpec((None, dexp, d), lambda j, e1, e2, nv, src, dst: (e1[j], 0, 0)),
            pl.BlockSpec((None, dexp, d), lambda j, e1, e2, nv, src, dst: (e2[j], 0, 0)),
        ],
        out_specs=pl.BlockSpec(memory_space=pl.ANY),
        scratch_shapes=[pltpu.VMEM((te, dx), F32), pltpu.VMEM((te, dx), F32),
                        pltpu.VMEM((te, d), F32), pltpu.VMEM((te, d), F32),
                        pltpu.SemaphoreType.DMA((2,)), pltpu.SemaphoreType.DMA((2,))],
    )
    return pl.pallas_call(
        functools.partial(_expert_kernel, te=te, d=d, dexp=dexp, n_tok=n),
        grid_spec=grid_spec,
        out_shape=jax.ShapeDtypeStruct((n + 2 * te, d), F32),
        compiler_params=_params(("arbitrary",)),
        name="experts",
    )(e1, e2, nvalid, src, dst, hx, wgu, wgu, wdn, wdn)


_PAIR_LO = np.array([lo for lo in range(EXPERTS_PER_GROUP) for hi in range(lo + 1, EXPERTS_PER_GROUP)],
                    np.int32)
_PAIR_HI = np.array([hi for lo in range(EXPERTS_PER_GROUP) for hi in range(lo + 1, EXPERTS_PER_GROUP)],
                    np.int32)


def _dispatch_tables(hx, cnt, *, te, n_tiles):
    n = hx.shape[0]
    d = hx.shape[1] - LANES
    cls = hx[:, d + META_CLS].astype(jnp.int32)
    rank = hx[:, d + META_RANK].astype(jnp.int32)
    counts = cnt[0, :N_CLASSES].astype(jnp.int32)
    tiles_c = (counts + te - 1) // te
    tile_end = jnp.cumsum(tiles_c)
    tile_start = tile_end - tiles_c
    dest = tile_start[cls] * te + rank
    slot = jnp.arange(n_tiles * te, dtype=jnp.int32)
    tok = jnp.full((n_tiles * te,), -1, jnp.int32).at[dest].set(jnp.arange(n, dtype=jnp.int32))
    src = jnp.maximum(tok, 0)
    dst = jnp.where(tok < 0, n + ((slot // te) % 2) * te + slot % te, tok)
    tile_ids = jnp.arange(n_tiles + 2, dtype=jnp.int32)
    tcls = jnp.minimum(jnp.sum((tile_ids[:, None] >= tile_end[None, :]).astype(jnp.int32), axis=1),
                       N_CLASSES - 1)
    grp = tcls // N_PAIRS
    pr = tcls % N_PAIRS
    e1 = grp * EXPERTS_PER_GROUP + jnp.asarray(_PAIR_LO)[pr]
    e2 = grp * EXPERTS_PER_GROUP + jnp.asarray(_PAIR_HI)[pr]
    return e1.astype(jnp.int32), e2.astype(jnp.int32), tile_end[-1:].astype(jnp.int32), src, dst


def _ple_kernel(x1_ref, y_ref, p_ref, wpg_ref, wple_ref, gfin_ref, o_ref, *, final):
    x2 = x1_ref[...] + y_ref[...]
    gate = _sigmoid(_dot(x2.astype(BF16), wpg_ref[...]))
    x3 = x2 + gate * _dot(p_ref[...].astype(BF16), wple_ref[...])
    o_ref[...] = _rms(x3, gfin_ref[...]) if final else x3


def _ple(x1, y, p2d, wpg, wple, gfin, *, seq, final):
    n, d = x1.shape
    tm = min(WIDE_TOKEN_TILE, seq)
    row = lambda w: pl.BlockSpec((tm, w), lambda i: (i, 0))
    return pl.pallas_call(
        functools.partial(_ple_kernel, final=final),
        grid=(n // tm,),
        in_specs=[row(d), row(d), row(p2d.shape[1]), _const_spec(wpg.shape), _const_spec(wple.shape),
                  _const_spec(gfin.shape)],
        out_specs=row(d),
        out_shape=jax.ShapeDtypeStruct((n, d), F32),
        compiler_params=_params(("arbitrary",)),
        name="ple",
    )(x1, y, p2d, wpg, wple, gfin)


def _pad_cols(w, width):
    return jnp.pad(w, ((0, 0), (0, width - w.shape[1])))


def _diff_weights(w_dq, w_dk, w_dv):
    d = w_dq.shape[0]
    z64 = jnp.zeros((d, LANES - DIFF_DH), w_dq.dtype)
    z128 = jnp.zeros((d, LANES), w_dq.dtype)
    wq, wk, wv = [], [], []
    for hm in range(2 * DIFF_HEADS):
        wq += [w_dq[:, hm * DIFF_DH:(hm + 1) * DIFF_DH], z64]
        wk += [w_dk[:, hm * DIFF_DH:(hm + 1) * DIFF_DH], z64]
    for hd in range(DIFF_HEADS):
        wv += [w_dv[:, hd * DIFF_DV:(hd + 1) * DIFF_DV], z128]
    cat = lambda parts: jnp.concatenate(parts, axis=1).astype(BF16)
    return cat(wq), cat(wk), cat(wv)


def _diff_constants(seq):
    slopes = [2.0 ** (-8.0 * (hd + 1.0) / DIFF_HEADS) for hd in range(DIFF_HEADS)]
    qc = np.zeros((1, 2 * DIFF_HEADS * LANES), np.float32)
    for hd in range(DIFF_HEADS):
        for mp in range(2):
            for dgt in range(N_POS_DIGITS):
                coef = slopes[hd] * float(POS_BASE ** dgt)
                mant = math.frexp(coef)[0] * 256.0
                assert mant == int(mant), "ALiBi coefficient must be exact in bf16"
                qc[0, (2 * hd + mp) * LANES + DIFF_DH + dgt] = coef
    assert seq <= POS_BASE ** N_POS_DIGITS
    pos = np.arange(seq)
    pf = np.zeros((seq, LANES), np.float32)
    for dgt in range(N_POS_DIGITS):
        pf[:, DIFF_DH + dgt] = (pos // (POS_BASE ** dgt)) % POS_BASE
    vc = np.zeros((1, DIFF_HEADS * 2 * LANES), np.float32)
    for hd in range(DIFF_HEADS):
        vc[0, hd * 2 * LANES + DIFF_DV] = 1.0
    iqk = np.zeros((2 * DIFF_HEADS * LANES, LANES), np.float32)
    iall = np.zeros((2 * DIFF_HEADS * LANES, LANES), np.float32)
    for hm in range(2 * DIFF_HEADS):
        iqk[hm * LANES:hm * LANES + DIFF_DH, hm] = 1.0
        iall[hm * LANES:(hm + 1) * LANES, hm] = 1.0
    return (jnp.asarray(qc), jnp.asarray(pf), jnp.asarray(vc), jnp.asarray(iqk, BF16),
            jnp.asarray(iall, BF16))


def kernel(x, p, g_mix, w_in, w_gla_a2, b_gla_a, g_gla_norm, w_o_gla, diff_lam, g_diff_norm, w_o_diff,
           conv_w, w_o_conv, w_out, g_ffn, w_router_group, b_router_group, w_router_expert,
           b_router_expert, w_expert_gate_up, w_expert_down, w_ple, w_ple_gate, g_final):
    batch, seq, d = x.shape
    depth = w_in.shape[0]
    n = batch * seq
    te = min(EXPERT_TILE, seq)
    n_tiles = n // te + N_CLASSES
    qc, pf, vc, iqk, iall = _diff_constants(seq)

    sizes = (GLA_QK, GLA_QK, GLA_V, GLA_V, GLA_RANK, DIFF_QK, DIFF_QK, DIFF_V,
             CONV_WIDTH, CONV_WIDTH, CONV_WIDTH, N_BRANCH * d)
    offs = np.concatenate([[0], np.cumsum(sizes)]).tolist()
    assert offs[-1] == w_in.shape[2]

    x2d = x.reshape(n, d)
    for layer in range(depth):
        wi = w_in[layer]
        seg = lambda a, b: wi[:, offs[a]:offs[b]]
        wgla = seg(0, 4).astype(BF16)
        wga = _pad_cols(seg(4, 5), LANES).astype(BF16)
        w2 = jnp.pad(w_gla_a2[layer], ((0, LANES - GLA_RANK), (0, 0))).astype(BF16)
        wq, wk, wv = _diff_weights(seg(5, 6), seg(6, 7), seg(7, 8))
        wconv = seg(8, 11).astype(BF16)
        wgates = seg(11, 12).astype(BF16)
        lam_init = 0.8 - 0.6 * math.exp(-0.3 * layer)

        gq, gk, la, gv, sg, yc, gates = _inproj_main(
            x2d, g_mix[layer][None, :], wgla, wga, w2, b_gla_a[layer][None, :], wconv,
            conv_w[layer], wgates, seq=seq)
        dq, dkt, dv, stats = _inproj_diff(x2d, g_mix[layer][None, :], wq, qc, wk, pf, wv, vc, iqk, iall,
                                          batch=batch, seq=seq)
        oa = _gla(gq, gk, la, gv, sg, g_gla_norm[layer][None, :], batch=batch, seq=seq)
        ob = _attn(dq, dkt, dv, stats, diff_lam[layer], g_diff_norm[layer][None, :],
                   batch=batch, seq=seq, lam_init=lam_init)

        wr = _pad_cols(jnp.concatenate([w_router_group[layer], w_router_expert[layer]], axis=1), LANES)
        wrh = wr.astype(BF16)
        wrl = (wr - wrh.astype(F32)).astype(BF16)
        br = _pad_cols(jnp.concatenate([b_router_group[layer], b_router_expert[layer]])[None, :], LANES)
        x1, hx, cnt = _merge_route(
            x2d, oa, ob, yc, gates, w_o_gla[layer].astype(BF16), w_o_diff[layer].astype(BF16),
            w_o_conv[layer].astype(BF16), w_out[layer].astype(BF16), g_ffn[layer][None, :],
            wrh, wrl, br, seq=seq)

        e1, e2, nvalid, src, dst = _dispatch_tables(hx, cnt, te=te, n_tiles=n_tiles)
        y = _experts(hx, w_expert_gate_up[layer].astype(BF16), w_expert_down[layer].astype(BF16),
                     e1, e2, nvalid, src, dst, te=te, n_tiles=n_tiles)

        x2d = _ple(x1, y, p[layer].reshape(n, p.shape[-1]), w_ple_gate[layer].astype(BF16),
                   w_ple[layer].astype(BF16), g_final[None, :], seq=seq,
                   final=(layer == depth - 1))
    return x2d.reshape(batch, seq, d)
```

```python
import functools
import math

import numpy as np
import jax
import jax.numpy as jnp
from jax import lax
from jax.experimental import pallas as pl
from jax.experimental.pallas import tpu as pltpu

F32 = jnp.float32
BF16 = jnp.bfloat16

EPS = 1e-6
GLA_HEADS = 4
GLA_DK = 64
GLA_DV = 128
GLA_RANK = 16
GLA_TAU = 16.0
GLA_QK = GLA_HEADS * GLA_DK
GLA_V = GLA_HEADS * GLA_DV
DIFF_HEADS = 4
DIFF_DH = 64
DIFF_DV = 2 * DIFF_DH
DIFF_QK = DIFF_HEADS * 2 * DIFF_DH
DIFF_V = DIFF_HEADS * DIFF_DV
CONV_WIDTH = 512
CONV_K = 3
N_GROUPS = 4
EXPERTS_PER_GROUP = 4
N_EXPERTS = N_GROUPS * EXPERTS_PER_GROUP
EPG_SHIFT = EXPERTS_PER_GROUP.bit_length() - 1
assert 1 << EPG_SHIFT == EXPERTS_PER_GROUP
N_PAIRS = EXPERTS_PER_GROUP * (EXPERTS_PER_GROUP - 1) // 2
N_CLASSES = N_GROUPS * N_PAIRS
N_BRANCH = 3

LANES = 128
POS_BASE = 128
N_POS_DIGITS = 3
GLA_SAFE_LOG_DECAY = 60.0

TOKEN_TILE = 512
WIDE_TOKEN_TILE = 1024
GLA_TILE = 256
ATTN_Q_TILE = 2048
ATTN_K_TILE = 1024
ATTN_ROWS = 128
MXU_COLS = 256
ATTN_SKIP_MARGIN = 110.0
EXPERT_TILE = 256
EXPERT_BUFS = 3
META_CLS = N_EXPERTS
META_RANK = N_EXPERTS + 1

VMEM_LIMIT = 56 * 1024 * 1024


def _const_spec(shape):
    nd = len(shape)
    return pl.BlockSpec(shape, lambda *_: (0,) * nd, pipeline_mode=pl.Buffered(1))


def _params(sem):
    return pltpu.CompilerParams(dimension_semantics=sem, vmem_limit_bytes=VMEM_LIMIT)


def _rms(xf, g):
    return xf * lax.rsqrt(jnp.mean(xf * xf, axis=-1, keepdims=True) + EPS) * g


def _sigmoid(v):
    return 1.0 / (1.0 + jnp.exp(-v))


def _dot(a, b):
    return jnp.dot(a, b, preferred_element_type=F32)


def _dot_nt(a, b):
    return lax.dot_general(a, b, (((1,), (1,)), ((), ())), preferred_element_type=F32)


def _dot_tn(a, b):
    return lax.dot_general(a, b, (((0,), (0,)), ((), ())), preferred_element_type=F32)


def _split_bf16(v):
    hi = v.astype(BF16)
    lo = (v - hi.astype(F32)).astype(BF16)
    return hi, lo


def _inproj_main_kernel(x_ref, g_ref, wgla_ref, wga_ref, w2_ref, ba_ref, wconv_ref, cw_ref,
                        wgates_ref, gq_ref, gk_ref, la_ref, gv_ref, sg_ref, yc_ref, gates_ref,
                        ubuf_ref, *, tiles_per_seq, tm):
    i = pl.program_id(0)
    h = _rms(x_ref[...], g_ref[...]).astype(BF16)

    z = _dot(h, wgla_ref[...])
    gq_ref[...] = z[:, :GLA_QK] * (GLA_DK ** -0.5)
    gk_ref[...] = z[:, GLA_QK:2 * GLA_QK]
    gv_ref[...] = z[:, 2 * GLA_QK:2 * GLA_QK + GLA_V].astype(BF16)
    gg = z[:, 2 * GLA_QK + GLA_V:]
    sg_ref[...] = (gg * _sigmoid(gg)).astype(BF16)

    ga = _dot(h, wga_ref[...])
    xa = _dot(ga.astype(BF16), w2_ref[...]) + ba_ref[...]
    log_sig = jnp.minimum(xa, 0.0) - jnp.log(1.0 + jnp.exp(-jnp.abs(xa)))
    la_ref[...] = log_sig * (1.0 / GLA_TAU)

    zc = _dot(h, wconv_ref[...])
    u = zc[:, 2 * CONV_WIDTH:] * zc[:, :CONV_WIDTH]

    @pl.when(i % tiles_per_seq == 0)
    def _():
        ubuf_ref[0:8, :] = jnp.zeros((8, CONV_WIDTH), F32)

    ubuf_ref[8:8 + tm, :] = u
    cw = cw_ref[...]
    y = (cw[0:1, :] * ubuf_ref[6:6 + tm, :] + cw[1:2, :] * ubuf_ref[7:7 + tm, :]
         + cw[2:3, :] * u)
    yc_ref[...] = (zc[:, CONV_WIDTH:2 * CONV_WIDTH] * y).astype(BF16)
    ubuf_ref[0:8, :] = ubuf_ref[tm:tm + 8, :]

    gates_ref[...] = _sigmoid(_dot(h, wgates_ref[...])).astype(BF16)


def _inproj_main(x2d, g, wgla, wga, w2, ba, wconv, cw, wgates, *, seq):
    n, d = x2d.shape
    tm = min(TOKEN_TILE, seq)
    row = lambda w: pl.BlockSpec((tm, w), lambda i: (i, 0))
    out_shape = (
        jax.ShapeDtypeStruct((n, GLA_QK), F32), jax.ShapeDtypeStruct((n, GLA_QK), F32),
        jax.ShapeDtypeStruct((n, GLA_QK), F32), jax.ShapeDtypeStruct((n, GLA_V), BF16),
        jax.ShapeDtypeStruct((n, GLA_V), BF16), jax.ShapeDtypeStruct((n, CONV_WIDTH), BF16),
        jax.ShapeDtypeStruct((n, N_BRANCH * d), BF16))
    return pl.pallas_call(
        functools.partial(_inproj_main_kernel, tiles_per_seq=seq // tm, tm=tm),
        grid=(n // tm,),
        in_specs=[row(d), _const_spec(g.shape), _const_spec(wgla.shape), _const_spec(wga.shape),
                  _const_spec(w2.shape), _const_spec(ba.shape), _const_spec(wconv.shape),
                  _const_spec(cw.shape), _const_spec(wgates.shape)],
        out_specs=(row(GLA_QK), row(GLA_QK), row(GLA_QK), row(GLA_V), row(GLA_V),
                   row(CONV_WIDTH), row(N_BRANCH * d)),
        out_shape=out_shape,
        scratch_shapes=[pltpu.VMEM((8 + tm, CONV_WIDTH), F32)],
        compiler_params=_params(("arbitrary",)),
        name="inproj_main",
    )(x2d, g, wgla, wga, w2, ba, wconv, cw, wgates)


def _inproj_diff_kernel(x_ref, g_ref, wq_ref, qc_ref, wk_ref, pf_ref, wv_ref, vc_ref, iqk_ref, iall_ref,
                        q_ref, kt_ref, v_ref, st_ref):
    h = _rms(x_ref[...], g_ref[...]).astype(BF16)
    qa = (_dot(h, wq_ref[...]) * (DIFF_DH ** -0.5) + qc_ref[...]).astype(BF16)
    q_ref[...] = qa
    zk = _dot(h, wk_ref[...])
    pf = pf_ref[...]
    zk = zk + jnp.concatenate([pf] * (zk.shape[1] // LANES), axis=1)
    kt_ref[...] = zk.T.astype(BF16)
    v_ref[...] = (_dot(h, wv_ref[...]) + vc_ref[...]).astype(BF16)

    qf = qa.astype(F32)
    kf = zk.astype(BF16).astype(F32)

    def block_sums(vals, ind_ref):
        return _dot(vals.astype(BF16), ind_ref[...])

    qn2 = jnp.max(block_sums(qf * qf, iqk_ref), axis=0, keepdims=True)
    kn2 = jnp.max(block_sums(kf * kf, iqk_ref), axis=0, keepdims=True)
    diag = jnp.min(block_sums(qf * kf, iall_ref), axis=0, keepdims=True)
    st_ref[...] = jnp.concatenate([qn2, kn2, diag, jnp.zeros((5, LANES), F32)], axis=0)


def _inproj_diff(x2d, g, wq, qc, wk, pf, wv, vc, iqk, iall, *, batch, seq):
    n, d = x2d.shape
    tm = min(TOKEN_TILE, seq)
    tps = seq // tm
    wq_cols, wk_cols, wv_cols = wq.shape[1], wk.shape[1], wv.shape[1]
    return pl.pallas_call(
        _inproj_diff_kernel,
        grid=(n // tm,),
        in_specs=[pl.BlockSpec((tm, d), lambda i: (i, 0)), _const_spec(g.shape),
                  _const_spec(wq.shape), _const_spec(qc.shape), _const_spec(wk.shape),
                  pl.BlockSpec((tm, LANES), lambda i: (i % tps, 0)),
                  _const_spec(wv.shape), _const_spec(vc.shape), _const_spec(iqk.shape),
                  _const_spec(iall.shape)],
        out_specs=(pl.BlockSpec((tm, wq_cols), lambda i: (i, 0)),
                   pl.BlockSpec((None, wk_cols, tm), lambda i: (i // tps, 0, i % tps)),
                   pl.BlockSpec((tm, wv_cols), lambda i: (i, 0)),
                   pl.BlockSpec((None, 8, LANES), lambda i: (i, 0, 0))),
        out_shape=(jax.ShapeDtypeStruct((n, wq_cols), BF16),
                   jax.ShapeDtypeStruct((batch, wk_cols, seq), BF16),
                   jax.ShapeDtypeStruct((n, wv_cols), BF16),
                   jax.ShapeDtypeStruct((n // tm, 8, LANES), F32)),
        compiler_params=_params(("arbitrary",)),
        name="inproj_diff",
    )(x2d, g, wq, qc, wk, pf, wv, vc, iqk, iall)


def _gla_kernel(q_ref, k_ref, la_ref, v_ref, sg_ref, gn_ref, o_ref, st_ref, a_ref, *, tg):
    @pl.when(pl.program_id(1) == 0)
    def _():
        st_ref[...] = jnp.zeros(st_ref.shape, F32)

    la = la_ref[...]
    la_hi, la_lo = _split_bf16(la)
    row = lax.broadcasted_iota(jnp.int32, (tg, tg), 0)
    col = lax.broadcasted_iota(jnp.int32, (tg, tg), 1)
    causal = col <= row

    def prefix(sel):
        sel = jnp.where(sel, 1.0, 0.0).astype(BF16)
        return _dot(sel, la_hi) + _dot(sel, la_lo)

    b = prefix(causal)
    btot = b[tg - 1:tg, :]
    q = q_ref[...]
    k = k_ref[...]
    qe = (q * jnp.exp(b)).astype(BF16)
    kd = (k * jnp.exp(btot - b)).astype(BF16)

    safe = jnp.min(btot) >= -GLA_SAFE_LOG_DECAY

    @pl.when(safe)
    def _():
        ke = (k * jnp.exp(-b)).astype(BF16)
        for hd in range(GLA_HEADS):
            sl = slice(hd * GLA_DK, (hd + 1) * GLA_DK)
            a = _dot_nt(qe[:, sl], ke[:, sl])
            a_ref[hd] = jnp.where(causal, a, 0.0).astype(BF16)

    @pl.when(jnp.logical_not(safe))
    def _():
        qb = q.astype(BF16)
        kb = k.astype(BF16)
        acc = [jnp.where(row == col, _dot_nt(qb[:, hd * GLA_DK:(hd + 1) * GLA_DK],
                                             kb[:, hd * GLA_DK:(hd + 1) * GLA_DK]), 0.0)
               for hd in range(GLA_HEADS)]
        rowv = lax.broadcasted_iota(jnp.int32, (tg, 1), 0)
        half = tg // 2
        while half >= 1:
            shift = int(math.log2(2 * half))
            mid = ((row >> shift) << shift) + (half - 1)
            r = prefix(col <= mid)
            second = (rowv & (2 * half - 1)) >= half
            fac = jnp.exp(jnp.where(second, b - r, r - b))
            ql = jnp.where(second, q * fac, 0.0).astype(BF16)
            kl = jnp.where(second, 0.0, k * fac).astype(BF16)
            same = (row >> shift) == (col >> shift)
            for hd in range(GLA_HEADS):
                sl = slice(hd * GLA_DK, (hd + 1) * GLA_DK)
                acc[hd] = acc[hd] + jnp.where(same, _dot_nt(ql[:, sl], kl[:, sl]), 0.0)
            half //= 2
        for hd in range(GLA_HEADS):
            a_ref[hd] = acc[hd].astype(BF16)

    v = v_ref[...]
    gn = gn_ref[...]
    decay = jnp.exp(btot)
    for hd in range(GLA_HEADS):
        sl = slice(hd * GLA_DK, (hd + 1) * GLA_DK)
        vs = slice(hd * GLA_DV, (hd + 1) * GLA_DV)
        st = st_ref[hd]
        o = _dot(a_ref[hd], v[:, vs]) + _dot_nt(qe[:, sl], st.astype(BF16))
        st_ref[hd] = st * decay[:, sl] + _dot_tn(v[:, vs], kd[:, sl])
        o_ref[:, vs] = (_rms(o, gn) * sg_ref[:, vs].astype(F32)).astype(BF16)


def _gla(gq, gk, la, gv, sg, gn, *, batch, seq):
    n = gq.shape[0]
    tg = min(GLA_TILE, seq)
    tps = seq // tg
    row = lambda w: pl.BlockSpec((tg, w), lambda b, j: (b * tps + j, 0))
    return pl.pallas_call(
        functools.partial(_gla_kernel, tg=tg),
        grid=(batch, tps),
        in_specs=[row(GLA_QK), row(GLA_QK), row(GLA_QK), row(GLA_V), row(GLA_V),
                  pl.BlockSpec(gn.shape, lambda b, j: (0, 0))],
        out_specs=row(GLA_V),
        out_shape=jax.ShapeDtypeStruct((n, GLA_V), BF16),
        scratch_shapes=[pltpu.VMEM((GLA_HEADS, GLA_DV, GLA_DK), F32),
                        pltpu.VMEM((GLA_HEADS, tg, tg), BF16)],
        compiler_params=_params(("arbitrary", "arbitrary")),
        name="gla",
    )(gq, gk, la, gv, sg, gn)


def _attn_kernel(qi_ref, kj_ref, kind_ref, first_ref, q_ref, kt_ref, v_ref, *rest, lam_init, tq, tk, steps):
    n_diag = tq // tk
    if n_diag == 2:
        kt2_ref, v2_ref, lam_ref, gn_ref, o_ref, m_ref, acc_ref = rest
    else:
        lam_ref, gn_ref, o_ref, m_ref, acc_ref = rest
    idx = pl.program_id(0) * steps + pl.program_id(1)
    kind = kind_ref[idx]
    ar = min(ATTN_ROWS, tq)

    @pl.when(first_ref[idx] == 1)
    def _():
        m_ref[...] = jnp.full(m_ref.shape, -jnp.inf, F32)
        acc_ref[...] = jnp.zeros(acc_ref.shape, F32)

    def chain(rb, mp, pieces):
        rows = slice(rb * ar, (rb + 1) * ar)
        q = q_ref[rows, mp * LANES:(mp + 1) * LANES]
        scores = []
        for ktr, _, c_end, rel in pieces:
            s = _dot(q, ktr[mp * LANES:(mp + 1) * LANES, 0:c_end])
            if rel is not None:
                row = lax.broadcasted_iota(jnp.int32, (ar, c_end), 0) + rel
                col = lax.broadcasted_iota(jnp.int32, (ar, c_end), 1)
                s = jnp.where(col <= row, s, -jnp.inf)
            scores.append(s)
        m_old = m_ref[mp, rows, :]
        m_new = m_old
        for s in scores:
            m_new = jnp.maximum(m_new, jnp.max(s, axis=1, keepdims=True))
        acc = jnp.exp(m_old - m_new) * acc_ref[mp, rows, :]
        for s, (_, vr, c_end, _) in zip(scores, pieces):
            acc = acc + _dot(jnp.exp((s - m_new).astype(BF16)), vr[0:c_end, :])
        acc_ref[mp, rows, :] = acc
        m_ref[mp, rows, :] = m_new

    def step(diagonal):
        tiles = [(kt_ref, v_ref), (kt2_ref, v2_ref)][:n_diag] if diagonal else [(kt_ref, v_ref)]
        for rb in range(tq // ar):
            pieces = []
            for d, (ktr, vr) in enumerate(tiles):
                rel = rb * ar - d * tk if diagonal else tk
                if rel + ar <= 0:
                    continue
                if rel >= tk - 1:
                    pieces.append((ktr, vr, tk, None))
                else:
                    pieces.append((ktr, vr, min(tk, -(-(rel + ar) // MXU_COLS) * MXU_COLS), rel))
            for mp in range(2):
                chain(rb, mp, pieces)

    @pl.when(kind == -1)
    def _():
        step(False)

    @pl.when(kind == 0)
    def _():
        step(True)
        lp = lam_ref[...]
        lam = (jnp.exp(jnp.sum(lp[0:1] * lp[1:2], axis=1, keepdims=True))
               - jnp.exp(jnp.sum(lp[2:3] * lp[3:4], axis=1, keepdims=True)) + lam_init)
        a1 = acc_ref[0]
        a2 = acc_ref[1]
        o = (a1[:, :DIFF_DV] / a1[:, DIFF_DV:DIFF_DV + 1]
             - lam * (a2[:, :DIFF_DV] / a2[:, DIFF_DV:DIFF_DV + 1]))
        o_ref[...] = (_rms(o, gn_ref[...]) * (1.0 - lam_init)).astype(BF16)


def _attn_tiles(seq):
    tk = min(ATTN_K_TILE, seq)
    tq = min(ATTN_Q_TILE, seq)
    return tq, tk


def _attn_steps(stats, *, batch, seq):
    tq, tk = _attn_tiles(seq)
    nq, nk, n_diag = seq // tq, seq // tk, tq // tk
    tm = min(TOKEN_TILE, seq)
    assert n_diag in (1, 2)
    pairs = [(i, j) for i in range(nq) for j in range(i * n_diag + 1)]
    qi_s = np.array([p[0] for p in pairs], np.int32)
    kj_s = np.array([p[1] for p in pairs], np.int32)
    kind_s = np.array([0 if j == i * n_diag else -1 for i, j in pairs], np.int32)
    steps = len(pairs)

    nhm = 2 * DIFF_HEADS
    per_tile = stats.reshape(batch, seq // tm, 8, LANES)
    q2 = jnp.max(per_tile[:, :, 0, :nhm].reshape(batch, nq, tq // tm, nhm), axis=2)
    k2 = jnp.max(per_tile[:, :, 1, :nhm].reshape(batch, nk, tk // tm, nhm), axis=2)
    dmin = jnp.min(per_tile[:, :, 2, :nhm].reshape(batch, nq, tq // tm, nhm), axis=2)
    slopes = np.repeat([2.0 ** (-8.0 * (hd + 1.0) / DIFF_HEADS) for hd in range(DIFF_HEADS)], 2)
    last_key = ((kj_s + 1) * tk - 1).astype(np.float32)
    bound = (jnp.sqrt(q2[:, qi_s, :] * k2[:, kj_s, :]) * 1.01
             + jnp.asarray(slopes, F32)[None, None, :] * jnp.asarray(last_key)[None, :, None])
    negligible = bound < dmin[:, qi_s, :] - ATTN_SKIP_MARGIN
    negligible = jnp.all(negligible.reshape(batch, steps, DIFF_HEADS, 2), axis=3)
    active = jnp.logical_or(jnp.asarray(kind_s >= 0)[None, :, None], jnp.logical_not(negligible))
    active = active.transpose(0, 2, 1).reshape(batch * DIFF_HEADS, steps)

    pos = jnp.arange(steps, dtype=jnp.int32)
    dest = jnp.cumsum(active.astype(jnp.int32), axis=1) - 1
    n_act = dest[:, -1:] + 1
    sel = jnp.logical_and(active[:, :, None], dest[:, :, None] == pos[None, None, :])
    pick = lambda tab: jnp.sum(jnp.where(sel, jnp.asarray(tab)[None, :, None], 0), axis=1)
    qi, kj, kind = pick(qi_s), pick(kj_s), pick(kind_s)
    idle = pos[None, :] >= n_act
    hold = lambda a: jnp.where(idle, jnp.take_along_axis(a, n_act - 1, axis=1), a)
    qi, kj = hold(qi), hold(kj)
    kind = jnp.where(idle, -2, kind)
    prev_qi = jnp.concatenate([jnp.full_like(qi[:, :1], -1), qi[:, :-1]], axis=1)
    first = jnp.logical_and(qi != prev_qi, jnp.logical_not(idle)).astype(jnp.int32)
    flat = lambda a: a.reshape(-1).astype(jnp.int32)
    return flat(qi), flat(kj), flat(kind), flat(first), steps


def _attn(dq, dkt, dv, stats, lam_p, gn, *, batch, seq, lam_init):
    tq, tk = _attn_tiles(seq)
    qi, kj, kind, first, steps = _attn_steps(stats, batch=batch, seq=seq)
    dq3 = dq.reshape(batch, seq, dq.shape[1])
    dv3 = dv.reshape(batch, seq, dv.shape[1])
    nh = DIFF_HEADS
    n_diag = tq // tk
    at = lambda g, t: g * steps + t
    last = lambda qi, g, t: n_diag * qi[at(g, t)] + n_diag - 1
    second = [
        pl.BlockSpec((None, 2 * LANES, tk), lambda g, t, qi, kj, kd, ft: (g // nh, g % nh, last(qi, g, t))),
        pl.BlockSpec((None, tk, 2 * LANES), lambda g, t, qi, kj, kd, ft: (g // nh, last(qi, g, t), g % nh)),
    ] if n_diag == 2 else []
    grid_spec = pltpu.PrefetchScalarGridSpec(
        num_scalar_prefetch=4,
        grid=(batch * nh, steps),
        in_specs=[
            pl.BlockSpec((None, tq, 2 * LANES), lambda g, t, qi, kj, kd, ft: (g // nh, qi[at(g, t)], g % nh)),
            pl.BlockSpec((None, 2 * LANES, tk), lambda g, t, qi, kj, kd, ft: (g // nh, g % nh, kj[at(g, t)])),
            pl.BlockSpec((None, tk, 2 * LANES), lambda g, t, qi, kj, kd, ft: (g // nh, kj[at(g, t)], g % nh)),
        ] + second + [
            pl.BlockSpec(lam_p.shape, lambda g, t, qi, kj, kd, ft: (0, 0)),
            pl.BlockSpec(gn.shape, lambda g, t, qi, kj, kd, ft: (0, 0)),
        ],
        out_specs=pl.BlockSpec((None, tq, DIFF_DV),
                               lambda g, t, qi, kj, kd, ft: (g // nh, qi[at(g, t)], g % nh)),
        scratch_shapes=[pltpu.VMEM((2, tq, 1), F32), pltpu.VMEM((2, tq, 2 * LANES), F32)],
    )
    out = pl.pallas_call(
        functools.partial(_attn_kernel, lam_init=lam_init, tq=tq, tk=tk, steps=steps),
        grid_spec=grid_spec,
        out_shape=jax.ShapeDtypeStruct((batch, seq, DIFF_V), BF16),
        compiler_params=_params(("arbitrary", "arbitrary")),
        name="diff_attn",
    )(qi, kj, kind, first, dq3, dkt, dv3, *([dkt, dv3] if n_diag == 2 else []), lam_p, gn)
    return out.reshape(batch * seq, DIFF_V)


def _merge_route_kernel(x_ref, oa_ref, ob_ref, oc_ref, gt_ref, woa_ref, wob_ref, woc_ref, wout_ref,
                        gf_ref, wrh_ref, wrl_ref, br_ref, x1_ref, hx_ref, cnt_ref, *, tm, d):
    i = pl.program_id(0)

    @pl.when(i == 0)
    def _():
        cnt_ref[...] = jnp.zeros(cnt_ref.shape, F32)

    ya = _dot(oa_ref[...], woa_ref[...])
    yb = _dot(ob_ref[...], wob_ref[...])
    yc = _dot(oc_ref[...], woc_ref[...])
    merged = (gt_ref[:, 0:d].astype(F32) * ya + gt_ref[:, d:2 * d].astype(F32) * yb
              + gt_ref[:, 2 * d:3 * d].astype(F32) * yc)
    x1 = x_ref[...] + _dot(merged.astype(BF16), wout_ref[...])
    x1_ref[...] = x1
    h2 = _rms(x1, gf_ref[...])
    hx_ref[:, 0:d] = h2

    h_hi, h_lo = _split_bf16(h2)
    wrh = wrh_ref[...]
    logits = _dot(h_hi, wrh) + _dot(h_lo, wrh) + _dot(h_hi, wrl_ref[...]) + br_ref[...]
    lane = lax.broadcasted_iota(jnp.int32, (tm, LANES), 1)
    ninf = -jnp.inf
    glog = jnp.where(lane < N_GROUPS, logits, ninf)
    gmax = jnp.max(glog, axis=1, keepdims=True)
    gi = jnp.min(jnp.where(glog == gmax, lane, LANES), axis=1, keepdims=True)
    gp = 1.0 / jnp.sum(jnp.exp(glog - gmax), axis=1, keepdims=True)
    eid = lane - N_GROUPS
    in_group = (eid >= 0) & (eid < N_EXPERTS) & ((eid >> EPG_SHIFT) == gi)
    w0 = jnp.where(in_group, logits, ninf)
    m1 = jnp.max(w0, axis=1, keepdims=True)
    i1 = jnp.min(jnp.where(w0 == m1, lane, LANES), axis=1, keepdims=True)
    w1 = jnp.where(lane == i1, ninf, w0)
    m2 = jnp.max(w1, axis=1, keepdims=True)
    i2 = jnp.min(jnp.where(w1 == m2, lane, LANES), axis=1, keepdims=True)
    e1 = i1 - N_GROUPS
    e2 = i2 - N_GROUPS
    tt = jnp.exp(m2 - m1)
    p1 = gp / (1.0 + tt)
    p2 = gp * tt / (1.0 + tt)
    comb = jnp.where(lane == e1, p1, 0.0) + jnp.where(lane == e2, p2, 0.0)

    lo = jnp.minimum(e1, e2) & (EXPERTS_PER_GROUP - 1)
    hi = jnp.maximum(e1, e2) & (EXPERTS_PER_GROUP - 1)
    pair = ((lo * (2 * EXPERTS_PER_GROUP - 1 - lo)) >> 1) + (hi - lo - 1)
    cls = gi * N_PAIRS + pair
    onehot = lane == cls
    row = lax.broadcasted_iota(jnp.int32, (tm, tm), 0)
    col = lax.broadcasted_iota(jnp.int32, (tm, tm), 1)
    before = jnp.where(col < row, 1.0, 0.0).astype(BF16)
    seen = _dot(before, jnp.where(onehot, 1.0, 0.0).astype(BF16)) + cnt_ref[0:1, :]
    rank = jnp.sum(jnp.where(onehot, seen, 0.0), axis=1, keepdims=True)
    cnt_ref[0:1, :] = cnt_ref[0:1, :] + jnp.sum(jnp.where(onehot, 1.0, 0.0), axis=0, keepdims=True)

    meta = comb + jnp.where(lane == META_CLS, cls.astype(F32), 0.0) \
        + jnp.where(lane == META_RANK, rank, 0.0)
    hx_ref[:, d:d + LANES] = meta


def _merge_route(x2d, oa, ob, oc, gates, woa, wob, woc, wout, gf, wrh, wrl, br, *, seq):
    n, d = x2d.shape
    tm = min(TOKEN_TILE, seq)
    row = lambda w: pl.BlockSpec((tm, w), lambda i: (i, 0))
    return pl.pallas_call(
        functools.partial(_merge_route_kernel, tm=tm, d=d),
        grid=(n // tm,),
        in_specs=[row(d), row(GLA_V), row(DIFF_V), row(CONV_WIDTH), row(N_BRANCH * d),
                  _const_spec(woa.shape), _const_spec(wob.shape), _const_spec(woc.shape),
                  _const_spec(wout.shape), _const_spec(gf.shape), _const_spec(wrh.shape),
                  _const_spec(wrl.shape), _const_spec(br.shape)],
        out_specs=(row(d), row(d + LANES), pl.BlockSpec((8, LANES), lambda i: (0, 0))),
        out_shape=(jax.ShapeDtypeStruct((n, d), F32), jax.ShapeDtypeStruct((n, d + LANES), F32),
                   jax.ShapeDtypeStruct((8, LANES), F32)),
        compiler_params=_params(("arbitrary",)),
        name="merge_route",
    )(x2d, oa, ob, oc, gates, woa, wob, woc, wout, gf, wrh, wrl, br)


def _expert_kernel(e1_ref, e2_ref, nv_ref, src_ref, dst_ref, hx_ref, wgu1_ref, wgu2_ref, wdn1_ref, wdn2_ref,
                   y_ref, *scratch, te, d, dexp, n_tok):
    nb = EXPERT_BUFS
    xbufs, ybufs, (gsem_ref, ssem_ref) = scratch[:nb], scratch[nb:2 * nb], scratch[2 * nb:]
    j = pl.program_id(0)
    nv = nv_ref[0]

    def gather_one(tile, r, par):
        pltpu.make_async_copy(hx_ref.at[pl.ds(src_ref[tile * te + r], 1)], xbufs[par].at[pl.ds(r, 1)],
                              gsem_ref.at[par]).start()

    def scatter_one(tile, r, par):
        pltpu.make_async_copy(ybufs[par].at[pl.ds(r, 1)], y_ref.at[pl.ds(dst_ref[tile * te + r], 1)],
                              ssem_ref.at[par]).start()

    def wait_gather(par):
        pltpu.make_async_copy(hx_ref.at[pl.ds(0, te)], xbufs[par], gsem_ref.at[par]).wait()

    def wait_scatter(par):
        pltpu.make_async_copy(ybufs[par], y_ref.at[pl.ds(0, te)], ssem_ref.at[par]).wait()

    def compute(par):
        xb = xbufs[par]
        xs = xb[:, 0:d].astype(BF16)
        meta = xb[:, d:d + LANES]
        lane = lax.broadcasted_iota(jnp.int32, (te, LANES), 1)
        out = jnp.zeros((te, d), F32)
        for e_ref, wgu_ref, wdn_ref in ((e1_ref, wgu1_ref, wdn1_ref), (e2_ref, wgu2_ref, wdn2_ref)):
            gu = _dot(xs, wgu_ref[...])
            gate = gu[:, :dexp]
            act = (gate * _sigmoid(gate) * gu[:, dexp:]).astype(BF16)
            wcol = jnp.sum(jnp.where(lane == e_ref[j], meta, 0.0), axis=1, keepdims=True)
            out = out + wcol * _dot(act, wdn_ref[...])
        ybufs[par][...] = out

    def looped(fn, tile, par):
        def body(r, c):
            fn(tile, r, par)
            return c
        lax.fori_loop(0, te, body, 0)

    steady = jnp.logical_and(j >= nb, j + nb - 1 < nv)
    for par in range(nb):
        mine = (j % nb) == par
        ahead = (par + nb - 1) % nb

        @pl.when(jnp.logical_and(mine, steady))
        def _(par=par, ahead=ahead):
            wait_gather(par)
            wait_scatter(par)
            compute(par)
            for r in range(te):
                scatter_one(j, r, par)
                gather_one(j + nb - 1, r, ahead)

        @pl.when(jnp.logical_and(mine, jnp.logical_not(steady)))
        def _(par=par, ahead=ahead):
            if par == 0:
                @pl.when(j == 0)
                def _():
                    ybufs[0][...] = jnp.zeros((te, d), F32)
                    for k in range(nb):
                        fill = pltpu.make_async_copy(ybufs[0], y_ref.at[pl.ds(n_tok + k * te, te)],
                                                     ssem_ref.at[0])
                        fill.start()
                        fill.wait()
                    for k in range(nb - 1):
                        @pl.when(k < nv)
                        def _(k=k):
                            looped(gather_one, k, k)

            @pl.when(j < nv)
            def _():
                wait_gather(par)

            @pl.when(jnp.logical_and(j >= nb, j - nb < nv))
            def _():
                wait_scatter(par)

            @pl.when(j < nv)
            def _():
                compute(par)
                looped(scatter_one, j, par)

            @pl.when(j + nb - 1 < nv)
            def _():
                looped(gather_one, j + nb - 1, ahead)


def _experts(hx, wgu, wdn, e1, e2, nvalid, src, dst, *, te, n_tiles):
    n, dx = hx.shape
    d = dx - LANES
    dexp = wdn.shape[1]
    grid_spec = pltpu.PrefetchScalarGridSpec(
        num_scalar_prefetch=5,
        grid=(n_tiles + EXPERT_BUFS,),
        in_specs=[
            pl.BlockSpec(memory_space=pl.ANY),
            pl.BlockSpec((None, d, 2 * dexp), lambda j, e1, e2, nv, src, dst: (e1[j], 0, 0)),
            pl.BlockSpec((None, d, 2 * dexp), lambda j, e1, e2, nv, src, dst: (e2[j], 0, 0)),
            pl.BlockSpec((None, dexp, d), lambda j, e1, e2, nv, src, dst: (e1[j], 0, 0)),
            pl.BlockSpec((None, dexp, d), lambda j, e1, e2, nv, src, dst: (e2[j], 0, 0)),
        ],
        out_specs=pl.BlockSpec(memory_space=pl.ANY),
        scratch_shapes=([pltpu.VMEM((te, dx), F32)] * EXPERT_BUFS + [pltpu.VMEM((te, d), F32)] * EXPERT_BUFS
                        + [pltpu.SemaphoreType.DMA((EXPERT_BUFS,)), pltpu.SemaphoreType.DMA((EXPERT_BUFS,))]),
    )
    return pl.pallas_call(
        functools.partial(_expert_kernel, te=te, d=d, dexp=dexp, n_tok=n),
        grid_spec=grid_spec,
        out_shape=jax.ShapeDtypeStruct((n + EXPERT_BUFS * te, d), F32),
        compiler_params=_params(("arbitrary",)),
        name="experts",
    )(e1, e2, nvalid, src, dst, hx, wgu, wgu, wdn, wdn)


_PAIR_LO = np.array([lo for lo in range(EXPERTS_PER_GROUP) for hi in range(lo + 1, EXPERTS_PER_GROUP)],
                    np.int32)
_PAIR_HI = np.array([hi for lo in range(EXPERTS_PER_GROUP) for hi in range(lo + 1, EXPERTS_PER_GROUP)],
                    np.int32)


def _dispatch_tables(hx, cnt, *, te, n_tiles):
    n = hx.shape[0]
    d = hx.shape[1] - LANES
    cls = hx[:, d + META_CLS].astype(jnp.int32)
    rank = hx[:, d + META_RANK].astype(jnp.int32)
    counts = cnt[0, :N_CLASSES].astype(jnp.int32)
    tiles_c = (counts + te - 1) // te
    tile_end = jnp.cumsum(tiles_c)
    tile_start = tile_end - tiles_c
    dest = tile_start[cls] * te + rank
    slot = jnp.arange(n_tiles * te, dtype=jnp.int32)
    tok = jnp.full((n_tiles * te,), -1, jnp.int32).at[dest].set(jnp.arange(n, dtype=jnp.int32))
    src = jnp.maximum(tok, 0)
    dst = jnp.where(tok < 0, n + ((slot // te) % EXPERT_BUFS) * te + slot % te, tok)
    tile_ids = jnp.arange(n_tiles + EXPERT_BUFS, dtype=jnp.int32)
    tcls = jnp.minimum(jnp.sum((tile_ids[:, None] >= tile_end[None, :]).astype(jnp.int32), axis=1),
                       N_CLASSES - 1)
    grp = tcls // N_PAIRS
    pr = tcls % N_PAIRS
    e1 = grp * EXPERTS_PER_GROUP + jnp.asarray(_PAIR_LO)[pr]
    e2 = grp * EXPERTS_PER_GROUP + jnp.asarray(_PAIR_HI)[pr]
    return e1.astype(jnp.int32), e2.astype(jnp.int32), tile_end[-1:].astype(jnp.int32), src, dst


def _ple_kernel(x1_ref, y_ref, p_ref, wpg_ref, wple_ref, gfin_ref, o_ref, *, final):
    x2 = x1_ref[...] + y_ref[...]
    gate = _sigmoid(_dot(x2.astype(BF16), wpg_ref[...]))
    x3 = x2 + gate * _dot(p_ref[...].astype(BF16), wple_ref[...])
    o_ref[...] = _rms(x3, gfin_ref[...]) if final else x3


def _ple(x1, y, p2d, wpg, wple, gfin, *, seq, final):
    n, d = x1.shape
    tm = min(WIDE_TOKEN_TILE, seq)
    row = lambda w: pl.BlockSpec((tm, w), lambda i: (i, 0))
    return pl.pallas_call(
        functools.partial(_ple_kernel, final=final),
        grid=(n // tm,),
        in_specs=[row(d), row(d), row(p2d.shape[1]), _const_spec(wpg.shape), _const_spec(wple.shape),
                  _const_spec(gfin.shape)],
        out_specs=row(d),
        out_shape=jax.ShapeDtypeStruct((n, d), F32),
        compiler_params=_params(("arbitrary",)),
        name="ple",
    )(x1, y, p2d, wpg, wple, gfin)


def _pad_cols(w, width):
    return jnp.pad(w, ((0, 0), (0, width - w.shape[1])))


def _diff_weights(w_dq, w_dk, w_dv):
    d = w_dq.shape[0]
    z64 = jnp.zeros((d, LANES - DIFF_DH), w_dq.dtype)
    z128 = jnp.zeros((d, LANES), w_dq.dtype)
    wq, wk, wv = [], [], []
    for hm in range(2 * DIFF_HEADS):
        wq += [w_dq[:, hm * DIFF_DH:(hm + 1) * DIFF_DH], z64]
        wk += [w_dk[:, hm * DIFF_DH:(hm + 1) * DIFF_DH], z64]
    for hd in range(DIFF_HEADS):
        wv += [w_dv[:, hd * DIFF_DV:(hd + 1) * DIFF_DV], z128]
    cat = lambda parts: jnp.concatenate(parts, axis=1).astype(BF16)
    return cat(wq), cat(wk), cat(wv)


def _diff_constants(seq):
    slopes = [2.0 ** (-8.0 * (hd + 1.0) / DIFF_HEADS) for hd in range(DIFF_HEADS)]
    qc = np.zeros((1, 2 * DIFF_HEADS * LANES), np.float32)
    for hd in range(DIFF_HEADS):
        for mp in range(2):
            for dgt in range(N_POS_DIGITS):
                coef = slopes[hd] * float(POS_BASE ** dgt)
                mant = math.frexp(coef)[0] * 256.0
                assert mant == int(mant), "ALiBi coefficient must be exact in bf16"
                qc[0, (2 * hd + mp) * LANES + DIFF_DH + dgt] = coef
    assert seq <= POS_BASE ** N_POS_DIGITS
    pos = np.arange(seq)
    pf = np.zeros((seq, LANES), np.float32)
    for dgt in range(N_POS_DIGITS):
        pf[:, DIFF_DH + dgt] = (pos // (POS_BASE ** dgt)) % POS_BASE
    vc = np.zeros((1, DIFF_HEADS * 2 * LANES), np.float32)
    for hd in range(DIFF_HEADS):
        vc[0, hd * 2 * LANES + DIFF_DV] = 1.0
    iqk = np.zeros((2 * DIFF_HEADS * LANES, LANES), np.float32)
    iall = np.zeros((2 * DIFF_HEADS * LANES, LANES), np.float32)
    for hm in range(2 * DIFF_HEADS):
        iqk[hm * LANES:hm * LANES + DIFF_DH, hm] = 1.0
        iall[hm * LANES:(hm + 1) * LANES, hm] = 1.0
    return (jnp.asarray(qc), jnp.asarray(pf), jnp.asarray(vc), jnp.asarray(iqk, BF16),
            jnp.asarray(iall, BF16))


def kernel(x, p, g_mix, w_in, w_gla_a2, b_gla_a, g_gla_norm, w_o_gla, diff_lam, g_diff_norm, w_o_diff,
           conv_w, w_o_conv, w_out, g_ffn, w_router_group, b_router_group, w_router_expert,
           b_router_expert, w_expert_gate_up, w_expert_down, w_ple, w_ple_gate, g_final):
    batch, seq, d = x.shape
    depth = w_in.shape[0]
    n = batch * seq
    te = min(EXPERT_TILE, seq)
    n_tiles = n // te + N_CLASSES
    qc, pf, vc, iqk, iall = _diff_constants(seq)

    sizes = (GLA_QK, GLA_QK, GLA_V, GLA_V, GLA_RANK, DIFF_QK, DIFF_QK, DIFF_V,
             CONV_WIDTH, CONV_WIDTH, CONV_WIDTH, N_BRANCH * d)
    offs = np.concatenate([[0], np.cumsum(sizes)]).tolist()
    assert offs[-1] == w_in.shape[2]

    x2d = x.reshape(n, d)
    for layer in range(depth):
        wi = w_in[layer]
        seg = lambda a, b: wi[:, offs[a]:offs[b]]
        wgla = seg(0, 4).astype(BF16)
        wga = _pad_cols(seg(4, 5), LANES).astype(BF16)
        w2 = jnp.pad(w_gla_a2[layer], ((0, LANES - GLA_RANK), (0, 0))).astype(BF16)
        wq, wk, wv = _diff_weights(seg(5, 6), seg(6, 7), seg(7, 8))
        wconv = seg(8, 11).astype(BF16)
        wgates = seg(11, 12).astype(BF16)
        lam_init = 0.8 - 0.6 * math.exp(-0.3 * layer)

        gq, gk, la, gv, sg, yc, gates = _inproj_main(
            x2d, g_mix[layer][None, :], wgla, wga, w2, b_gla_a[layer][None, :], wconv,
            conv_w[layer], wgates, seq=seq)
        dq, dkt, dv, stats = _inproj_diff(x2d, g_mix[layer][None, :], wq, qc, wk, pf, wv, vc, iqk, iall,
                                          batch=batch, seq=seq)
        oa = _gla(gq, gk, la, gv, sg, g_gla_norm[layer][None, :], batch=batch, seq=seq)
        ob = _attn(dq, dkt, dv, stats, diff_lam[layer], g_diff_norm[layer][None, :],
                   batch=batch, seq=seq, lam_init=lam_init)

        wr = _pad_cols(jnp.concatenate([w_router_group[layer], w_router_expert[layer]], axis=1), LANES)
        wrh = wr.astype(BF16)
        wrl = (wr - wrh.astype(F32)).astype(BF16)
        br = _pad_cols(jnp.concatenate([b_router_group[layer], b_router_expert[layer]])[None, :], LANES)
        x1, hx, cnt = _merge_route(
            x2d, oa, ob, yc, gates, w_o_gla[layer].astype(BF16), w_o_diff[layer].astype(BF16),
            w_o_conv[layer].astype(BF16), w_out[layer].astype(BF16), g_ffn[layer][None, :],
            wrh, wrl, br, seq=seq)

        e1, e2, nvalid, src, dst = _dispatch_tables(hx, cnt, te=te, n_tiles=n_tiles)
        y = _experts(hx, w_expert_gate_up[layer].astype(BF16), w_expert_down[layer].astype(BF16),
                     e1, e2, nvalid, src, dst, te=te, n_tiles=n_tiles)

        x2d = _ple(x1, y, p[layer].reshape(n, p.shape[-1]), w_ple_gate[layer].astype(BF16),
                   w_ple[layer].astype(BF16), g_final[None, :], seq=seq,
                   final=(layer == depth - 1))
    return x2d.reshape(batch, seq, d)
```

```python
import functools
import math

import numpy as np
import jax
import jax.numpy as jnp
from jax import lax
from jax.experimental import pallas as pl
from jax.experimental.pallas import tpu as pltpu

F32 = jnp.float32
BF16 = jnp.bfloat16

EPS = 1e-6
GLA_HEADS = 4
GLA_DK = 64
GLA_DV = 128
GLA_RANK = 16
GLA_TAU = 16.0
GLA_QK = GLA_HEADS * GLA_DK
GLA_V = GLA_HEADS * GLA_DV
DIFF_HEADS = 4
DIFF_DH = 64
DIFF_DV = 2 * DIFF_DH
DIFF_QK = DIFF_HEADS * 2 * DIFF_DH
DIFF_V = DIFF_HEADS * DIFF_DV
CONV_WIDTH = 512
CONV_K = 3
N_GROUPS = 4
EXPERTS_PER_GROUP = 4
N_EXPERTS = N_GROUPS * EXPERTS_PER_GROUP
EPG_SHIFT = EXPERTS_PER_GROUP.bit_length() - 1
assert 1 << EPG_SHIFT == EXPERTS_PER_GROUP
N_PAIRS = EXPERTS_PER_GROUP * (EXPERTS_PER_GROUP - 1) // 2
N_CLASSES = N_GROUPS * N_PAIRS
N_BRANCH = 3

LANES = 128
POS_BASE = 128
N_POS_DIGITS = 3
GLA_SAFE_LOG_DECAY = 60.0

TOKEN_TILE = 512
WIDE_TOKEN_TILE = 1024
GLA_TILE = 256
ATTN_Q_TILE = 2048
ATTN_K_TILE = 1024
ATTN_ROWS = 128
MXU_COLS = 256
ATTN_SKIP_MARGIN = 110.0
EXPERT_TILE = 256
EXPERT_BUFS = 3
META_CLS = N_EXPERTS
META_RANK = N_EXPERTS + 1

VMEM_LIMIT = 56 * 1024 * 1024


def _const_spec(shape):
    nd = len(shape)
    return pl.BlockSpec(shape, lambda *_: (0,) * nd, pipeline_mode=pl.Buffered(1))


def _params(sem):
    return pltpu.CompilerParams(dimension_semantics=sem, vmem_limit_bytes=VMEM_LIMIT)


def _rms(xf, g):
    return xf * lax.rsqrt(jnp.mean(xf * xf, axis=-1, keepdims=True) + EPS) * g


def _sigmoid(v):
    return 1.0 / (1.0 + jnp.exp(-v))


def _dot(a, b):
    return jnp.dot(a, b, preferred_element_type=F32)


def _dot_nt(a, b):
    return lax.dot_general(a, b, (((1,), (1,)), ((), ())), preferred_element_type=F32)


def _dot_tn(a, b):
    return lax.dot_general(a, b, (((0,), (0,)), ((), ())), preferred_element_type=F32)


def _split_bf16(v):
    hi = v.astype(BF16)
    lo = (v - hi.astype(F32)).astype(BF16)
    return hi, lo


def _inproj_main_kernel(x_ref, g_ref, wgla_ref, wga_ref, w2_ref, ba_ref, wconv_ref, cw_ref,
                        wgates_ref, gq_ref, gk_ref, la_ref, gv_ref, sg_ref, yc_ref, gates_ref,
                        ubuf_ref, *, tiles_per_seq, tm):
    i = pl.program_id(0)
    h = _rms(x_ref[...], g_ref[...]).astype(BF16)

    z = _dot(h, wgla_ref[...])
    gq_ref[...] = z[:, :GLA_QK] * (GLA_DK ** -0.5)
    gk_ref[...] = z[:, GLA_QK:2 * GLA_QK]
    gv_ref[...] = z[:, 2 * GLA_QK:2 * GLA_QK + GLA_V].astype(BF16)
    gg = z[:, 2 * GLA_QK + GLA_V:]
    sg_ref[...] = (gg * _sigmoid(gg)).astype(BF16)

    ga = _dot(h, wga_ref[...])
    xa = _dot(ga.astype(BF16), w2_ref[...]) + ba_ref[...]
    log_sig = jnp.minimum(xa, 0.0) - jnp.log(1.0 + jnp.exp(-jnp.abs(xa)))
    la_ref[...] = log_sig * (1.0 / GLA_TAU)

    zc = _dot(h, wconv_ref[...])
    u = zc[:, 2 * CONV_WIDTH:] * zc[:, :CONV_WIDTH]

    @pl.when(i % tiles_per_seq == 0)
    def _():
        ubuf_ref[0:8, :] = jnp.zeros((8, CONV_WIDTH), F32)

    ubuf_ref[8:8 + tm, :] = u
    cw = cw_ref[...]
    y = (cw[0:1, :] * ubuf_ref[6:6 + tm, :] + cw[1:2, :] * ubuf_ref[7:7 + tm, :]
         + cw[2:3, :] * u)
    yc_ref[...] = (zc[:, CONV_WIDTH:2 * CONV_WIDTH] * y).astype(BF16)
    ubuf_ref[0:8, :] = ubuf_ref[tm:tm + 8, :]

    gates_ref[...] = _sigmoid(_dot(h, wgates_ref[...])).astype(BF16)


def _inproj_main(x2d, g, wgla, wga, w2, ba, wconv, cw, wgates, *, seq):
    n, d = x2d.shape
    tm = min(TOKEN_TILE, seq)
    row = lambda w: pl.BlockSpec((tm, w), lambda i: (i, 0))
    out_shape = (
        jax.ShapeDtypeStruct((n, GLA_QK), F32), jax.ShapeDtypeStruct((n, GLA_QK), F32),
        jax.ShapeDtypeStruct((n, GLA_QK), F32), jax.ShapeDtypeStruct((n, GLA_V), BF16),
        jax.ShapeDtypeStruct((n, GLA_V), BF16), jax.ShapeDtypeStruct((n, CONV_WIDTH), BF16),
        jax.ShapeDtypeStruct((n, N_BRANCH * d), BF16))
    return pl.pallas_call(
        functools.partial(_inproj_main_kernel, tiles_per_seq=seq // tm, tm=tm),
        grid=(n // tm,),
        in_specs=[row(d), _const_spec(g.shape), _const_spec(wgla.shape), _const_spec(wga.shape),
                  _const_spec(w2.shape), _const_spec(ba.shape), _const_spec(wconv.shape),
                  _const_spec(cw.shape), _const_spec(wgates.shape)],
        out_specs=(row(GLA_QK), row(GLA_QK), row(GLA_QK), row(GLA_V), row(GLA_V),
                   row(CONV_WIDTH), row(N_BRANCH * d)),
        out_shape=out_shape,
        scratch_shapes=[pltpu.VMEM((8 + tm, CONV_WIDTH), F32)],
        compiler_params=_params(("arbitrary",)),
        name="inproj_main",
    )(x2d, g, wgla, wga, w2, ba, wconv, cw, wgates)


def _inproj_diff_kernel(x_ref, g_ref, wq_ref, qc_ref, wk_ref, pf_ref, wv_ref, vc_ref, iqk_ref, iall_ref,
                        q_ref, kt_ref, v_ref, st_ref):
    h = _rms(x_ref[...], g_ref[...]).astype(BF16)
    qa = (_dot(h, wq_ref[...]) * (DIFF_DH ** -0.5) + qc_ref[...]).astype(BF16)
    q_ref[...] = qa
    zk = _dot(h, wk_ref[...])
    pf = pf_ref[...]
    zk = zk + jnp.concatenate([pf] * (zk.shape[1] // LANES), axis=1)
    kt_ref[...] = zk.T.astype(BF16)
    v_ref[...] = (_dot(h, wv_ref[...]) + vc_ref[...]).astype(BF16)

    qf = qa.astype(F32)
    kf = zk.astype(BF16).astype(F32)

    def block_sums(vals, ind_ref):
        return _dot(vals.astype(BF16), ind_ref[...])

    qn2 = jnp.max(block_sums(qf * qf, iqk_ref), axis=0, keepdims=True)
    kn2 = jnp.max(block_sums(kf * kf, iqk_ref), axis=0, keepdims=True)
    diag = jnp.min(block_sums(qf * kf, iall_ref), axis=0, keepdims=True)
    st_ref[...] = jnp.concatenate([qn2, kn2, diag, jnp.zeros((5, LANES), F32)], axis=0)


def _inproj_diff(x2d, g, wq, qc, wk, pf, wv, vc, iqk, iall, *, batch, seq):
    n, d = x2d.shape
    tm = min(TOKEN_TILE, seq)
    tps = seq // tm
    wq_cols, wk_cols, wv_cols = wq.shape[1], wk.shape[1], wv.shape[1]
    return pl.pallas_call(
        _inproj_diff_kernel,
        grid=(n // tm,),
        in_specs=[pl.BlockSpec((tm, d), lambda i: (i, 0)), _const_spec(g.shape),
                  _const_spec(wq.shape), _const_spec(qc.shape), _const_spec(wk.shape),
                  pl.BlockSpec((tm, LANES), lambda i: (i % tps, 0)),
                  _const_spec(wv.shape), _const_spec(vc.shape), _const_spec(iqk.shape),
                  _const_spec(iall.shape)],
        out_specs=(pl.BlockSpec((tm, wq_cols), lambda i: (i, 0)),
                   pl.BlockSpec((None, wk_cols, tm), lambda i: (i // tps, 0, i % tps)),
                   pl.BlockSpec((tm, wv_cols), lambda i: (i, 0)),
                   pl.BlockSpec((None, 8, LANES), lambda i: (i, 0, 0))),
        out_shape=(jax.ShapeDtypeStruct((n, wq_cols), BF16),
                   jax.ShapeDtypeStruct((batch, wk_cols, seq), BF16),
                   jax.ShapeDtypeStruct((n, wv_cols), BF16),
                   jax.ShapeDtypeStruct((n // tm, 8, LANES), F32)),
        compiler_params=_params(("arbitrary",)),
        name="inproj_diff",
    )(x2d, g, wq, qc, wk, pf, wv, vc, iqk, iall)


def _gla_kernel(q_ref, k_ref, la_ref, v_ref, sg_ref, gn_ref, o_ref, st_ref, a_ref, *, tg):
    @pl.when(pl.program_id(1) == 0)
    def _():
        st_ref[...] = jnp.zeros(st_ref.shape, F32)

    la = la_ref[...]
    la_hi, la_lo = _split_bf16(la)
    row = lax.broadcasted_iota(jnp.int32, (tg, tg), 0)
    col = lax.broadcasted_iota(jnp.int32, (tg, tg), 1)
    causal = col <= row

    def prefix(sel):
        sel = jnp.where(sel, 1.0, 0.0).astype(BF16)
        return _dot(sel, la_hi) + _dot(sel, la_lo)

    b = prefix(causal)
    btot = b[tg - 1:tg, :]
    q = q_ref[...]
    k = k_ref[...]
    qe = (q * jnp.exp(b)).astype(BF16)
    kd = (k * jnp.exp(btot - b)).astype(BF16)

    safe = jnp.min(btot) >= -GLA_SAFE_LOG_DECAY

    @pl.when(safe)
    def _():
        ke = (k * jnp.exp(-b)).astype(BF16)
        for hd in range(GLA_HEADS):
            sl = slice(hd * GLA_DK, (hd + 1) * GLA_DK)
            a = _dot_nt(qe[:, sl], ke[:, sl])
            a_ref[hd] = jnp.where(causal, a, 0.0).astype(BF16)

    @pl.when(jnp.logical_not(safe))
    def _():
        qb = q.astype(BF16)
        kb = k.astype(BF16)
        acc = [jnp.where(row == col, _dot_nt(qb[:, hd * GLA_DK:(hd + 1) * GLA_DK],
                                             kb[:, hd * GLA_DK:(hd + 1) * GLA_DK]), 0.0)
               for hd in range(GLA_HEADS)]
        rowv = lax.broadcasted_iota(jnp.int32, (tg, 1), 0)
        half = tg // 2
        while half >= 1:
            shift = int(math.log2(2 * half))
            mid = ((row >> shift) << shift) + (half - 1)
            r = prefix(col <= mid)
            second = (rowv & (2 * half - 1)) >= half
            fac = jnp.exp(jnp.where(second, b - r, r - b))
            ql = jnp.where(second, q * fac, 0.0).astype(BF16)
            kl = jnp.where(second, 0.0, k * fac).astype(BF16)
            same = (row >> shift) == (col >> shift)
            for hd in range(GLA_HEADS):
                sl = slice(hd * GLA_DK, (hd + 1) * GLA_DK)
                acc[hd] = acc[hd] + jnp.where(same, _dot_nt(ql[:, sl], kl[:, sl]), 0.0)
            half //= 2
        for hd in range(GLA_HEADS):
            a_ref[hd] = acc[hd].astype(BF16)

    v = v_ref[...]
    gn = gn_ref[...]
    decay = jnp.exp(btot)
    for hd in range(GLA_HEADS):
        sl = slice(hd * GLA_DK, (hd + 1) * GLA_DK)
        vs = slice(hd * GLA_DV, (hd + 1) * GLA_DV)
        st = st_ref[hd]
        o = _dot(a_ref[hd], v[:, vs]) + _dot_nt(qe[:, sl], st.astype(BF16))
        st_ref[hd] = st * decay[:, sl] + _dot_tn(v[:, vs], kd[:, sl])
        o_ref[:, vs] = (_rms(o, gn) * sg_ref[:, vs].astype(F32)).astype(BF16)


def _gla(gq, gk, la, gv, sg, gn, *, batch, seq):
    n = gq.shape[0]
    tg = min(GLA_TILE, seq)
    tps = seq // tg
    row = lambda w: pl.BlockSpec((tg, w), lambda b, j: (b * tps + j, 0))
    return pl.pallas_call(
        functools.partial(_gla_kernel, tg=tg),
        grid=(batch, tps),
        in_specs=[row(GLA_QK), row(GLA_QK), row(GLA_QK), row(GLA_V), row(GLA_V),
                  pl.BlockSpec(gn.shape, lambda b, j: (0, 0))],
        out_specs=row(GLA_V),
        out_shape=jax.ShapeDtypeStruct((n, GLA_V), BF16),
        scratch_shapes=[pltpu.VMEM((GLA_HEADS, GLA_DV, GLA_DK), F32),
                        pltpu.VMEM((GLA_HEADS, tg, tg), BF16)],
        compiler_params=_params(("arbitrary", "arbitrary")),
        name="gla",
    )(gq, gk, la, gv, sg, gn)


def _attn_kernel(qi_ref, kj_ref, kind_ref, first_ref, q_ref, kt_ref, v_ref, *rest, lam_init, tq, tk, steps):
    n_diag = tq // tk
    if n_diag == 2:
        kt2_ref, v2_ref, lam_ref, gn_ref, o_ref, m_ref, acc_ref = rest
    else:
        lam_ref, gn_ref, o_ref, m_ref, acc_ref = rest
    idx = pl.program_id(0) * steps + pl.program_id(1)
    kind = kind_ref[idx]
    ar = min(ATTN_ROWS, tq)

    @pl.when(first_ref[idx] == 1)
    def _():
        m_ref[...] = jnp.full(m_ref.shape, -jnp.inf, F32)
        acc_ref[...] = jnp.zeros(acc_ref.shape, F32)

    def chain(rb, mp, pieces):
        rows = slice(rb * ar, (rb + 1) * ar)
        q = q_ref[rows, mp * LANES:(mp + 1) * LANES]
        scores = []
        for ktr, _, c_end, rel in pieces:
            s = _dot(q, ktr[mp * LANES:(mp + 1) * LANES, 0:c_end])
            if rel is not None:
                row = lax.broadcasted_iota(jnp.int32, (ar, c_end), 0) + rel
                col = lax.broadcasted_iota(jnp.int32, (ar, c_end), 1)
                s = jnp.where(col <= row, s, -jnp.inf)
            scores.append(s)
        m_old = m_ref[mp, rows, :]
        m_new = m_old
        for s in scores:
            m_new = jnp.maximum(m_new, jnp.max(s, axis=1, keepdims=True))
        acc = jnp.exp(m_old - m_new) * acc_ref[mp, rows, :]
        for s, (_, vr, c_end, _) in zip(scores, pieces):
            acc = acc + _dot(jnp.exp((s - m_new).astype(BF16)), vr[0:c_end, :])
        acc_ref[mp, rows, :] = acc
        m_ref[mp, rows, :] = m_new

    def step(diagonal):
        tiles = [(kt_ref, v_ref), (kt2_ref, v2_ref)][:n_diag] if diagonal else [(kt_ref, v_ref)]
        for rb in range(tq // ar):
            pieces = []
            for d, (ktr, vr) in enumerate(tiles):
                rel = rb * ar - d * tk if diagonal else tk
                if rel + ar <= 0:
                    continue
                if rel >= tk - 1:
                    pieces.append((ktr, vr, tk, None))
                else:
                    pieces.append((ktr, vr, min(tk, -(-(rel + ar) // MXU_COLS) * MXU_COLS), rel))
            for mp in range(2):
                chain(rb, mp, pieces)

    @pl.when(kind == -1)
    def _():
        step(False)

    @pl.when(kind == 0)
    def _():
        step(True)
        lp = lam_ref[...]
        lam = (jnp.exp(jnp.sum(lp[0:1] * lp[1:2], axis=1, keepdims=True))
               - jnp.exp(jnp.sum(lp[2:3] * lp[3:4], axis=1, keepdims=True)) + lam_init)
        a1 = acc_ref[0]
        a2 = acc_ref[1]
        o = (a1[:, :DIFF_DV] / a1[:, DIFF_DV:DIFF_DV + 1]
             - lam * (a2[:, :DIFF_DV] / a2[:, DIFF_DV:DIFF_DV + 1]))
        o_ref[...] = (_rms(o, gn_ref[...]) * (1.0 - lam_init)).astype(BF16)


def _attn_tiles(seq):
    tk = min(ATTN_K_TILE, seq)
    tq = min(ATTN_Q_TILE, seq)
    return tq, tk


def _attn_steps(stats, *, batch, seq):
    tq, tk = _attn_tiles(seq)
    nq, nk, n_diag = seq // tq, seq // tk, tq // tk
    tm = min(TOKEN_TILE, seq)
    assert n_diag in (1, 2)
    pairs = [(i, j) for i in range(nq) for j in range(i * n_diag + 1)]
    qi_s = np.array([p[0] for p in pairs], np.int32)
    kj_s = np.array([p[1] for p in pairs], np.int32)
    kind_s = np.array([0 if j == i * n_diag else -1 for i, j in pairs], np.int32)
    steps = len(pairs)

    nhm = 2 * DIFF_HEADS
    per_tile = stats.reshape(batch, seq // tm, 8, LANES)
    q2 = jnp.max(per_tile[:, :, 0, :nhm].reshape(batch, nq, tq // tm, nhm), axis=2)
    k2 = jnp.max(per_tile[:, :, 1, :nhm].reshape(batch, nk, tk // tm, nhm), axis=2)
    dmin = jnp.min(per_tile[:, :, 2, :nhm].reshape(batch, nq, tq // tm, nhm), axis=2)
    slopes = np.repeat([2.0 ** (-8.0 * (hd + 1.0) / DIFF_HEADS) for hd in range(DIFF_HEADS)], 2)
    last_key = ((kj_s + 1) * tk - 1).astype(np.float32)
    bound = (jnp.sqrt(q2[:, qi_s, :] * k2[:, kj_s, :]) * 1.01
             + jnp.asarray(slopes, F32)[None, None, :] * jnp.asarray(last_key)[None, :, None])
    negligible = bound < dmin[:, qi_s, :] - ATTN_SKIP_MARGIN
    negligible = jnp.all(negligible.reshape(batch, steps, DIFF_HEADS, 2), axis=3)
    active = jnp.logical_or(jnp.asarray(kind_s >= 0)[None, :, None], jnp.logical_not(negligible))
    active = active.transpose(0, 2, 1).reshape(batch * DIFF_HEADS, steps)

    pos = jnp.arange(steps, dtype=jnp.int32)
    dest = jnp.cumsum(active.astype(jnp.int32), axis=1) - 1
    n_act = dest[:, -1:] + 1
    sel = jnp.logical_and(active[:, :, None], dest[:, :, None] == pos[None, None, :])
    pick = lambda tab: jnp.sum(jnp.where(sel, jnp.asarray(tab)[None, :, None], 0), axis=1)
    qi, kj, kind = pick(qi_s), pick(kj_s), pick(kind_s)
    idle = pos[None, :] >= n_act
    hold = lambda a: jnp.where(idle, jnp.take_along_axis(a, n_act - 1, axis=1), a)
    qi, kj = hold(qi), hold(kj)
    kind = jnp.where(idle, -2, kind)
    prev_qi = jnp.concatenate([jnp.full_like(qi[:, :1], -1), qi[:, :-1]], axis=1)
    first = jnp.logical_and(qi != prev_qi, jnp.logical_not(idle)).astype(jnp.int32)
    flat = lambda a: a.reshape(-1).astype(jnp.int32)
    return flat(qi), flat(kj), flat(kind), flat(first), steps


def _attn(dq, dkt, dv, stats, lam_p, gn, *, batch, seq, lam_init):
    tq, tk = _attn_tiles(seq)
    qi, kj, kind, first, steps = _attn_steps(stats, batch=batch, seq=seq)
    dq3 = dq.reshape(batch, seq, dq.shape[1])
    dv3 = dv.reshape(batch, seq, dv.shape[1])
    nh = DIFF_HEADS
    n_diag = tq // tk
    at = lambda g, t: g * steps + t
    last = lambda qi, g, t: n_diag * qi[at(g, t)] + n_diag - 1
    second = [
        pl.BlockSpec((None, 2 * LANES, tk), lambda g, t, qi, kj, kd, ft: (g // nh, g % nh, last(qi, g, t))),
        pl.BlockSpec((None, tk, 2 * LANES), lambda g, t, qi, kj, kd, ft: (g // nh, last(qi, g, t), g % nh)),
    ] if n_diag == 2 else []
    grid_spec = pltpu.PrefetchScalarGridSpec(
        num_scalar_prefetch=4,
        grid=(batch * nh, steps),
        in_specs=[
            pl.BlockSpec((None, tq, 2 * LANES), lambda g, t, qi, kj, kd, ft: (g // nh, qi[at(g, t)], g % nh)),
            pl.BlockSpec((None, 2 * LANES, tk), lambda g, t, qi, kj, kd, ft: (g // nh, g % nh, kj[at(g, t)])),
            pl.BlockSpec((None, tk, 2 * LANES), lambda g, t, qi, kj, kd, ft: (g // nh, kj[at(g, t)], g % nh)),
        ] + second + [
            pl.BlockSpec(lam_p.shape, lambda g, t, qi, kj, kd, ft: (0, 0)),
            pl.BlockSpec(gn.shape, lambda g, t, qi, kj, kd, ft: (0, 0)),
        ],
        out_specs=pl.BlockSpec((None, tq, DIFF_DV),
                               lambda g, t, qi, kj, kd, ft: (g // nh, qi[at(g, t)], g % nh)),
        scratch_shapes=[pltpu.VMEM((2, tq, 1), F32), pltpu.VMEM((2, tq, 2 * LANES), F32)],
    )
    out = pl.pallas_call(
        functools.partial(_attn_kernel, lam_init=lam_init, tq=tq, tk=tk, steps=steps),
        grid_spec=grid_spec,
        out_shape=jax.ShapeDtypeStruct((batch, seq, DIFF_V), BF16),
        compiler_params=_params(("arbitrary", "arbitrary")),
        name="diff_attn",
    )(qi, kj, kind, first, dq3, dkt, dv3, *([dkt, dv3] if n_diag == 2 else []), lam_p, gn)
    return out.reshape(batch * seq, DIFF_V)


def _merge_route_kernel(x_ref, oa_ref, ob_ref, oc_ref, gt_ref, woa_ref, wob_ref, woc_ref, wout_ref,
                        gf_ref, wrh_ref, wrl_ref, br_ref, x1_ref, hx_ref, cnt_ref, *, tm, d):
    i = pl.program_id(0)

    @pl.when(i == 0)
    def _():
        cnt_ref[...] = jnp.zeros(cnt_ref.shape, F32)

    ya = _dot(oa_ref[...], woa_ref[...])
    yb = _dot(ob_ref[...], wob_ref[...])
    yc = _dot(oc_ref[...], woc_ref[...])
    merged = (gt_ref[:, 0:d].astype(F32) * ya + gt_ref[:, d:2 * d].astype(F32) * yb
              + gt_ref[:, 2 * d:3 * d].astype(F32) * yc)
    x1 = x_ref[...] + _dot(merged.astype(BF16), wout_ref[...])
    x1_ref[...] = x1
    h2 = _rms(x1, gf_ref[...])
    hx_ref[:, 0:d] = h2

    h_hi, h_lo = _split_bf16(h2)
    both = _dot(h_hi, wrl_ref[...])
    logits = both[:, :LANES] + both[:, LANES:] + _dot(h_lo, wrh_ref[...]) + br_ref[...]
    lane = lax.broadcasted_iota(jnp.int32, (tm, LANES), 1)
    ninf = -jnp.inf
    glog = jnp.where(lane < N_GROUPS, logits, ninf)
    gmax = jnp.max(glog, axis=1, keepdims=True)
    gi = jnp.min(jnp.where(glog == gmax, lane, LANES), axis=1, keepdims=True)
    gp = 1.0 / jnp.sum(jnp.exp(glog - gmax), axis=1, keepdims=True)
    eid = lane - N_GROUPS
    in_group = (eid >= 0) & (eid < N_EXPERTS) & ((eid >> EPG_SHIFT) == gi)
    w0 = jnp.where(in_group, logits, ninf)
    m1 = jnp.max(w0, axis=1, keepdims=True)
    i1 = jnp.min(jnp.where(w0 == m1, lane, LANES), axis=1, keepdims=True)
    w1 = jnp.where(lane == i1, ninf, w0)
    m2 = jnp.max(w1, axis=1, keepdims=True)
    i2 = jnp.min(jnp.where(w1 == m2, lane, LANES), axis=1, keepdims=True)
    e1 = i1 - N_GROUPS
    e2 = i2 - N_GROUPS
    tt = jnp.exp(m2 - m1)
    p1 = gp / (1.0 + tt)
    p2 = gp * tt / (1.0 + tt)
    comb = jnp.where(lane == e1, p1, 0.0) + jnp.where(lane == e2, p2, 0.0)

    lo = jnp.minimum(e1, e2) & (EXPERTS_PER_GROUP - 1)
    hi = jnp.maximum(e1, e2) & (EXPERTS_PER_GROUP - 1)
    pair = ((lo * (2 * EXPERTS_PER_GROUP - 1 - lo)) >> 1) + (hi - lo - 1)
    cls = gi * N_PAIRS + pair
    onehot = lane == cls
    row = lax.broadcasted_iota(jnp.int32, (tm, tm), 0)
    col = lax.broadcasted_iota(jnp.int32, (tm, tm), 1)
    before = jnp.where(col < row, 1.0, 0.0).astype(BF16)
    seen = _dot(before, jnp.where(onehot, 1.0, 0.0).astype(BF16)) + cnt_ref[0:1, :]
    rank = jnp.sum(jnp.where(onehot, seen, 0.0), axis=1, keepdims=True)
    cnt_ref[0:1, :] = cnt_ref[0:1, :] + jnp.sum(jnp.where(onehot, 1.0, 0.0), axis=0, keepdims=True)

    meta = comb + jnp.where(lane == META_CLS, cls.astype(F32), 0.0) \
        + jnp.where(lane == META_RANK, rank, 0.0)
    hx_ref[:, d:d + LANES] = meta


def _merge_route(x2d, oa, ob, oc, gates, woa, wob, woc, wout, gf, wrh, wrl, br, *, seq):
    n, d = x2d.shape
    tm = min(TOKEN_TILE, seq)
    row = lambda w: pl.BlockSpec((tm, w), lambda i: (i, 0))
    return pl.pallas_call(
        functools.partial(_merge_route_kernel, tm=tm, d=d),
        grid=(n // tm,),
        in_specs=[row(d), row(GLA_V), row(DIFF_V), row(CONV_WIDTH), row(N_BRANCH * d),
                  _const_spec(woa.shape), _const_spec(wob.shape), _const_spec(woc.shape),
                  _const_spec(wout.shape), _const_spec(gf.shape), _const_spec(wrh.shape),
                  _const_spec(wrl.shape), _const_spec(br.shape)],
        out_specs=(row(d), row(d + LANES), pl.BlockSpec((8, LANES), lambda i: (0, 0))),
        out_shape=(jax.ShapeDtypeStruct((n, d), F32), jax.ShapeDtypeStruct((n, d + LANES), F32),
                   jax.ShapeDtypeStruct((8, LANES), F32)),
        compiler_params=_params(("arbitrary",)),
        name="merge_route",
    )(x2d, oa, ob, oc, gates, woa, wob, woc, wout, gf, wrh, wrl, br)


def _expert_kernel(e1_ref, e2_ref, nv_ref, src_ref, dst_ref, hx_ref, wgu1_ref, wgu2_ref, wdn1_ref, wdn2_ref,
                   y_ref, *scratch, te, d, dexp, n_tok):
    nb = EXPERT_BUFS
    xbufs, ybufs, (gsem_ref, ssem_ref) = scratch[:nb], scratch[nb:2 * nb], scratch[2 * nb:]
    j = pl.program_id(0)
    nv = nv_ref[0]

    def gather_one(tile, r, par):
        pltpu.make_async_copy(hx_ref.at[pl.ds(src_ref[tile * te + r], 1)], xbufs[par].at[pl.ds(r, 1)],
                              gsem_ref.at[par]).start()

    def scatter_one(tile, r, par):
        pltpu.make_async_copy(ybufs[par].at[pl.ds(r, 1)], y_ref.at[pl.ds(dst_ref[tile * te + r], 1)],
                              ssem_ref.at[par]).start()

    def wait_gather(par):
        pltpu.make_async_copy(hx_ref.at[pl.ds(0, te)], xbufs[par], gsem_ref.at[par]).wait()

    def wait_scatter(par):
        pltpu.make_async_copy(ybufs[par], y_ref.at[pl.ds(0, te)], ssem_ref.at[par]).wait()

    def expert(par, which):
        e_ref, wgu_ref, wdn_ref = ((e1_ref, wgu1_ref, wdn1_ref), (e2_ref, wgu2_ref, wdn2_ref))[which]
        xb = xbufs[par]
        lane = lax.broadcasted_iota(jnp.int32, (te, LANES), 1)
        gu = _dot(xb[:, 0:d].astype(BF16), wgu_ref[...])
        gate = gu[:, :dexp]
        act = (gate * _sigmoid(gate) * gu[:, dexp:]).astype(BF16)
        wcol = jnp.sum(jnp.where(lane == e_ref[j], xb[:, d:d + LANES], 0.0), axis=1, keepdims=True)
        return wcol * _dot(act, wdn_ref[...])

    def compute(par):
        ybufs[par][...] = expert(par, 0) + expert(par, 1)

    def looped(fn, tile, par):
        def body(r, c):
            fn(tile, r, par)
            return c
        lax.fori_loop(0, te, body, 0)

    steady = jnp.logical_and(j >= nb, j + nb - 1 < nv)
    for par in range(nb):
        mine = (j % nb) == par
        ahead = (par + nb - 1) % nb

        @pl.when(jnp.logical_and(mine, steady))
        def _(par=par, ahead=ahead):
            wait_gather(par)
            wait_scatter(par)

            @pl.when(nv > 0)
            def _():
                ybufs[par][...] = expert(par, 0)
                for r in range(te):
                    gather_one(j + nb - 1, r, ahead)

            @pl.when(nv > 1)
            def _():
                ybufs[par][...] += expert(par, 1)
                for r in range(te):
                    scatter_one(j, r, par)

        @pl.when(jnp.logical_and(mine, jnp.logical_not(steady)))
        def _(par=par, ahead=ahead):
            if par == 0:
                @pl.when(j == 0)
                def _():
                    ybufs[0][...] = jnp.zeros((te, d), F32)
                    for k in range(nb):
                        fill = pltpu.make_async_copy(ybufs[0], y_ref.at[pl.ds(n_tok + k * te, te)],
                                                     ssem_ref.at[0])
                        fill.start()
                        fill.wait()
                    for k in range(nb - 1):
                        @pl.when(k < nv)
                        def _(k=k):
                            looped(gather_one, k, k)

            @pl.when(j < nv)
            def _():
                wait_gather(par)

            @pl.when(jnp.logical_and(j >= nb, j - nb < nv))
            def _():
                wait_scatter(par)

            @pl.when(j < nv)
            def _():
                compute(par)
                looped(scatter_one, j, par)

            @pl.when(j + nb - 1 < nv)
            def _():
                looped(gather_one, j + nb - 1, ahead)


def _experts(hx, wgu, wdn, e1, e2, nvalid, src, dst, *, te, n_tiles):
    n, dx = hx.shape
    d = dx - LANES
    dexp = wdn.shape[1]
    grid_spec = pltpu.PrefetchScalarGridSpec(
        num_scalar_prefetch=5,
        grid=(n_tiles + EXPERT_BUFS,),
        in_specs=[
            pl.BlockSpec(memory_space=pl.ANY),
            pl.BlockSpec((None, d, 2 * dexp), lambda j, e1, e2, nv, src, dst: (e1[j], 0, 0)),
            pl.BlockSpec((None, d, 2 * dexp), lambda j, e1, e2, nv, src, dst: (e2[j], 0, 0)),
            pl.BlockSpec((None, dexp, d), lambda j, e1, e2, nv, src, dst: (e1[j], 0, 0)),
            pl.BlockSpec((None, dexp, d), lambda j, e1, e2, nv, src, dst: (e2[j], 0, 0)),
        ],
        out_specs=pl.BlockSpec(memory_space=pl.ANY),
        scratch_shapes=([pltpu.VMEM((te, dx), F32)] * EXPERT_BUFS + [pltpu.VMEM((te, d), F32)] * EXPERT_BUFS
                        + [pltpu.SemaphoreType.DMA((EXPERT_BUFS,)), pltpu.SemaphoreType.DMA((EXPERT_BUFS,))]),
    )
    return pl.pallas_call(
        functools.partial(_expert_kernel, te=te, d=d, dexp=dexp, n_tok=n),
        grid_spec=grid_spec,
        out_shape=jax.ShapeDtypeStruct((n + EXPERT_BUFS * te, d), F32),
        compiler_params=_params(("arbitrary",)),
        name="experts",
    )(e1, e2, nvalid, src, dst, hx, wgu, wgu, wdn, wdn)


_PAIR_LO = np.array([lo for lo in range(EXPERTS_PER_GROUP) for hi in range(lo + 1, EXPERTS_PER_GROUP)],
                    np.int32)
_PAIR_HI = np.array([hi for lo in range(EXPERTS_PER_GROUP) for hi in range(lo + 1, EXPERTS_PER_GROUP)],
                    np.int32)


def _dispatch_tables(hx, cnt, *, te, n_tiles):
    n = hx.shape[0]
    d = hx.shape[1] - LANES
    cls = hx[:, d + META_CLS].astype(jnp.int32)
    rank = hx[:, d + META_RANK].astype(jnp.int32)
    counts = cnt[0, :N_CLASSES].astype(jnp.int32)
    tiles_c = (counts + te - 1) // te
    tile_end = jnp.cumsum(tiles_c)
    tile_start = tile_end - tiles_c
    start_of = jnp.dot(jax.nn.one_hot(cls, N_CLASSES, dtype=F32), tile_start.astype(F32),
                       precision=lax.Precision.HIGHEST)
    dest = start_of.astype(jnp.int32) * te + rank
    slot = jnp.arange(n_tiles * te, dtype=jnp.int32)
    tok = jnp.full((n_tiles * te,), -1, jnp.int32).at[dest].set(jnp.arange(n, dtype=jnp.int32))
    src = jnp.maximum(tok, 0)
    dst = jnp.where(tok < 0, n + ((slot // te) % EXPERT_BUFS) * te + slot % te, tok)
    tile_ids = jnp.arange(n_tiles + EXPERT_BUFS, dtype=jnp.int32)
    tcls = jnp.minimum(jnp.sum((tile_ids[:, None] >= tile_end[None, :]).astype(jnp.int32), axis=1),
                       N_CLASSES - 1)
    grp = tcls // N_PAIRS
    pr = tcls % N_PAIRS
    e1 = grp * EXPERTS_PER_GROUP + jnp.asarray(_PAIR_LO)[pr]
    e2 = grp * EXPERTS_PER_GROUP + jnp.asarray(_PAIR_HI)[pr]
    return e1.astype(jnp.int32), e2.astype(jnp.int32), tile_end[-1:].astype(jnp.int32), src, dst


def _ple_kernel(x1_ref, y_ref, p_ref, wpg_ref, wple_ref, gfin_ref, o_ref, *, final):
    x2 = x1_ref[...] + y_ref[...]
    gate = _sigmoid(_dot(x2.astype(BF16), wpg_ref[...]))
    x3 = x2 + gate * _dot(p_ref[...].astype(BF16), wple_ref[...])
    o_ref[...] = _rms(x3, gfin_ref[...]) if final else x3


def _ple(x1, y, p2d, wpg, wple, gfin, *, seq, final):
    n, d = x1.shape
    tm = min(WIDE_TOKEN_TILE, seq)
    row = lambda w: pl.BlockSpec((tm, w), lambda i: (i, 0))
    return pl.pallas_call(
        functools.partial(_ple_kernel, final=final),
        grid=(n // tm,),
        in_specs=[row(d), row(d), row(p2d.shape[1]), _const_spec(wpg.shape), _const_spec(wple.shape),
                  _const_spec(gfin.shape)],
        out_specs=row(d),
        out_shape=jax.ShapeDtypeStruct((n, d), F32),
        compiler_params=_params(("arbitrary",)),
        name="ple",
    )(x1, y, p2d, wpg, wple, gfin)


def _pad_cols(w, width):
    return jnp.pad(w, ((0, 0), (0, width - w.shape[1])))


def _diff_weights(w_dq, w_dk, w_dv):
    d = w_dq.shape[0]
    z64 = jnp.zeros((d, LANES - DIFF_DH), w_dq.dtype)
    z128 = jnp.zeros((d, LANES), w_dq.dtype)
    wq, wk, wv = [], [], []
    for hm in range(2 * DIFF_HEADS):
        wq += [w_dq[:, hm * DIFF_DH:(hm + 1) * DIFF_DH], z64]
        wk += [w_dk[:, hm * DIFF_DH:(hm + 1) * DIFF_DH], z64]
    for hd in range(DIFF_HEADS):
        wv += [w_dv[:, hd * DIFF_DV:(hd + 1) * DIFF_DV], z128]
    cat = lambda parts: jnp.concatenate(parts, axis=1).astype(BF16)
    return cat(wq), cat(wk), cat(wv)


def _diff_constants(seq):
    slopes = [2.0 ** (-8.0 * (hd + 1.0) / DIFF_HEADS) for hd in range(DIFF_HEADS)]
    qc = np.zeros((1, 2 * DIFF_HEADS * LANES), np.float32)
    for hd in range(DIFF_HEADS):
        for mp in range(2):
            for dgt in range(N_POS_DIGITS):
                coef = slopes[hd] * float(POS_BASE ** dgt)
                mant = math.frexp(coef)[0] * 256.0
                assert mant == int(mant), "ALiBi coefficient must be exact in bf16"
                qc[0, (2 * hd + mp) * LANES + DIFF_DH + dgt] = coef
    assert seq <= POS_BASE ** N_POS_DIGITS
    pos = np.arange(seq)
    pf = np.zeros((seq, LANES), np.float32)
    for dgt in range(N_POS_DIGITS):
        pf[:, DIFF_DH + dgt] = (pos // (POS_BASE ** dgt)) % POS_BASE
    vc = np.zeros((1, DIFF_HEADS * 2 * LANES), np.float32)
    for hd in range(DIFF_HEADS):
        vc[0, hd * 2 * LANES + DIFF_DV] = 1.0
    iqk = np.zeros((2 * DIFF_HEADS * LANES, LANES), np.float32)
    iall = np.zeros((2 * DIFF_HEADS * LANES, LANES), np.float32)
    for hm in range(2 * DIFF_HEADS):
        iqk[hm * LANES:hm * LANES + DIFF_DH, hm] = 1.0
        iall[hm * LANES:(hm + 1) * LANES, hm] = 1.0
    return (jnp.asarray(qc), jnp.asarray(pf), jnp.asarray(vc), jnp.asarray(iqk, BF16),
            jnp.asarray(iall, BF16))


def kernel(x, p, g_mix, w_in, w_gla_a2, b_gla_a, g_gla_norm, w_o_gla, diff_lam, g_diff_norm, w_o_diff,
           conv_w, w_o_conv, w_out, g_ffn, w_router_group, b_router_group, w_router_expert,
           b_router_expert, w_expert_gate_up, w_expert_down, w_ple, w_ple_gate, g_final):
    batch, seq, d = x.shape
    depth = w_in.shape[0]
    n = batch * seq
    te = min(EXPERT_TILE, seq)
    n_tiles = n // te + N_CLASSES
    qc, pf, vc, iqk, iall = _diff_constants(seq)

    sizes = (GLA_QK, GLA_QK, GLA_V, GLA_V, GLA_RANK, DIFF_QK, DIFF_QK, DIFF_V,
             CONV_WIDTH, CONV_WIDTH, CONV_WIDTH, N_BRANCH * d)
    offs = np.concatenate([[0], np.cumsum(sizes)]).tolist()
    assert offs[-1] == w_in.shape[2]

    x2d = x.reshape(n, d)
    for layer in range(depth):
        wi = w_in[layer]
        seg = lambda a, b: wi[:, offs[a]:offs[b]]
        wgla = seg(0, 4).astype(BF16)
        wga = _pad_cols(seg(4, 5), LANES).astype(BF16)
        w2 = jnp.pad(w_gla_a2[layer], ((0, LANES - GLA_RANK), (0, 0))).astype(BF16)
        wq, wk, wv = _diff_weights(seg(5, 6), seg(6, 7), seg(7, 8))
        wconv = seg(8, 11).astype(BF16)
        wgates = seg(11, 12).astype(BF16)
        lam_init = 0.8 - 0.6 * math.exp(-0.3 * layer)

        gq, gk, la, gv, sg, yc, gates = _inproj_main(
            x2d, g_mix[layer][None, :], wgla, wga, w2, b_gla_a[layer][None, :], wconv,
            conv_w[layer], wgates, seq=seq)
        dq, dkt, dv, stats = _inproj_diff(x2d, g_mix[layer][None, :], wq, qc, wk, pf, wv, vc, iqk, iall,
                                          batch=batch, seq=seq)
        oa = _gla(gq, gk, la, gv, sg, g_gla_norm[layer][None, :], batch=batch, seq=seq)
        ob = _attn(dq, dkt, dv, stats, diff_lam[layer], g_diff_norm[layer][None, :],
                   batch=batch, seq=seq, lam_init=lam_init)

        wr = _pad_cols(jnp.concatenate([w_router_group[layer], w_router_expert[layer]], axis=1), LANES)
        wrh = wr.astype(BF16)
        wrl = jnp.concatenate([wrh, (wr - wrh.astype(F32)).astype(BF16)], axis=1)
        br = _pad_cols(jnp.concatenate([b_router_group[layer], b_router_expert[layer]])[None, :], LANES)
        x1, hx, cnt = _merge_route(
            x2d, oa, ob, yc, gates, w_o_gla[layer].astype(BF16), w_o_diff[layer].astype(BF16),
            w_o_conv[layer].astype(BF16), w_out[layer].astype(BF16), g_ffn[layer][None, :],
            wrh, wrl, br, seq=seq)

        e1, e2, nvalid, src, dst = _dispatch_tables(hx, cnt, te=te, n_tiles=n_tiles)
        y = _experts(hx, w_expert_gate_up[layer].astype(BF16), w_expert_down[layer].astype(BF16),
                     e1, e2, nvalid, src, dst, te=te, n_tiles=n_tiles)

        x2d = _ple(x1, y, p[layer].reshape(n, p.shape[-1]), w_ple_gate[layer].astype(BF16),
                   w_ple[layer].astype(BF16), g_final[None, :], seq=seq,
                   final=(layer == depth - 1))
    return x2d.reshape(batch, seq, d)
```

```python
import functools
import math

import numpy as np
import jax
import jax.numpy as jnp
from jax import lax
from jax.experimental import pallas as pl
from jax.experimental.pallas import tpu as pltpu

F32 = jnp.float32
BF16 = jnp.bfloat16

EPS = 1e-6
GLA_HEADS = 4
GLA_DK = 64
GLA_DV = 128
GLA_RANK = 16
GLA_TAU = 16.0
GLA_QK = GLA_HEADS * GLA_DK
GLA_V = GLA_HEADS * GLA_DV
DIFF_HEADS = 4
DIFF_DH = 64
DIFF_DV = 2 * DIFF_DH
DIFF_QK = DIFF_HEADS * 2 * DIFF_DH
DIFF_V = DIFF_HEADS * DIFF_DV
CONV_WIDTH = 512
CONV_K = 3
N_GROUPS = 4
EXPERTS_PER_GROUP = 4
N_EXPERTS = N_GROUPS * EXPERTS_PER_GROUP
EPG_SHIFT = EXPERTS_PER_GROUP.bit_length() - 1
assert 1 << EPG_SHIFT == EXPERTS_PER_GROUP
N_PAIRS = EXPERTS_PER_GROUP * (EXPERTS_PER_GROUP - 1) // 2
N_CLASSES = N_GROUPS * N_PAIRS
N_BRANCH = 3

LANES = 128
POS_BASE = 128
N_POS_DIGITS = 3
GLA_SAFE_LOG_DECAY = 60.0

TOKEN_TILE = 512
WIDE_TOKEN_TILE = 1024
GLA_TILE = 256
ATTN_Q_TILE = 2048
ATTN_K_TILE = 1024
ATTN_ROWS = 128
MXU_COLS = 256
ATTN_SKIP_MARGIN = 110.0
EXPERT_TILE = 256
EXPERT_BUFS = 3
META_CLS = N_EXPERTS
META_RANK = N_EXPERTS + 1

VMEM_LIMIT = 56 * 1024 * 1024


def _const_spec(shape):
    nd = len(shape)
    return pl.BlockSpec(shape, lambda *_: (0,) * nd, pipeline_mode=pl.Buffered(1))


def _params(sem):
    return pltpu.CompilerParams(dimension_semantics=sem, vmem_limit_bytes=VMEM_LIMIT)


def _rms(xf, g):
    return xf * lax.rsqrt(jnp.mean(xf * xf, axis=-1, keepdims=True) + EPS) * g


def _sigmoid(v):
    return 1.0 / (1.0 + jnp.exp(-v))


def _dot(a, b):
    return jnp.dot(a, b, preferred_element_type=F32)


def _dot_nt(a, b):
    return lax.dot_general(a, b, (((1,), (1,)), ((), ())), preferred_element_type=F32)


def _dot_tn(a, b):
    return lax.dot_general(a, b, (((0,), (0,)), ((), ())), preferred_element_type=F32)


def _split_bf16(v):
    hi = v.astype(BF16)
    lo = (v - hi.astype(F32)).astype(BF16)
    return hi, lo


def _inproj_main_kernel(x_ref, g_ref, wgla_ref, wga_ref, w2_ref, ba_ref, wconv_ref, cw_ref,
                        wgates_ref, gq_ref, gk_ref, la_ref, gv_ref, sg_ref, yc_ref, gates_ref,
                        ubuf_ref, *, tiles_per_seq, tm):
    i = pl.program_id(0)
    h = _rms(x_ref[...], g_ref[...]).astype(BF16)

    z = _dot(h, wgla_ref[...])
    gq_ref[...] = z[:, :GLA_QK] * (GLA_DK ** -0.5)
    gk_ref[...] = z[:, GLA_QK:2 * GLA_QK]
    gv_ref[...] = z[:, 2 * GLA_QK:2 * GLA_QK + GLA_V].astype(BF16)
    gg = z[:, 2 * GLA_QK + GLA_V:]
    sg_ref[...] = (gg * _sigmoid(gg)).astype(BF16)

    ga = _dot(h, wga_ref[...])
    xa = _dot(ga.astype(BF16), w2_ref[...]) + ba_ref[...]
    log_sig = jnp.minimum(xa, 0.0) - jnp.log(1.0 + jnp.exp(-jnp.abs(xa)))
    la_ref[...] = log_sig * (1.0 / GLA_TAU)

    zc = _dot(h, wconv_ref[...])
    u = zc[:, 2 * CONV_WIDTH:] * zc[:, :CONV_WIDTH]

    @pl.when(i % tiles_per_seq == 0)
    def _():
        ubuf_ref[0:8, :] = jnp.zeros((8, CONV_WIDTH), F32)

    ubuf_ref[8:8 + tm, :] = u
    cw = cw_ref[...]
    y = (cw[0:1, :] * ubuf_ref[6:6 + tm, :] + cw[1:2, :] * ubuf_ref[7:7 + tm, :]
         + cw[2:3, :] * u)
    yc_ref[...] = (zc[:, CONV_WIDTH:2 * CONV_WIDTH] * y).astype(BF16)
    ubuf_ref[0:8, :] = ubuf_ref[tm:tm + 8, :]

    gates_ref[...] = _sigmoid(_dot(h, wgates_ref[...])).astype(BF16)


def _inproj_main(x2d, g, wgla, wga, w2, ba, wconv, cw, wgates, *, seq):
    n, d = x2d.shape
    tm = min(TOKEN_TILE, seq)
    row = lambda w: pl.BlockSpec((tm, w), lambda i: (i, 0))
    out_shape = (
        jax.ShapeDtypeStruct((n, GLA_QK), F32), jax.ShapeDtypeStruct((n, GLA_QK), F32),
        jax.ShapeDtypeStruct((n, GLA_QK), F32), jax.ShapeDtypeStruct((n, GLA_V), BF16),
        jax.ShapeDtypeStruct((n, GLA_V), BF16), jax.ShapeDtypeStruct((n, CONV_WIDTH), BF16),
        jax.ShapeDtypeStruct((n, N_BRANCH * d), BF16))
    return pl.pallas_call(
        functools.partial(_inproj_main_kernel, tiles_per_seq=seq // tm, tm=tm),
        grid=(n // tm,),
        in_specs=[row(d), _const_spec(g.shape), _const_spec(wgla.shape), _const_spec(wga.shape),
                  _const_spec(w2.shape), _const_spec(ba.shape), _const_spec(wconv.shape),
                  _const_spec(cw.shape), _const_spec(wgates.shape)],
        out_specs=(row(GLA_QK), row(GLA_QK), row(GLA_QK), row(GLA_V), row(GLA_V),
                   row(CONV_WIDTH), row(N_BRANCH * d)),
        out_shape=out_shape,
        scratch_shapes=[pltpu.VMEM((8 + tm, CONV_WIDTH), F32)],
        compiler_params=_params(("arbitrary",)),
        name="inproj_main",
    )(x2d, g, wgla, wga, w2, ba, wconv, cw, wgates)


def _inproj_diff_kernel(x_ref, g_ref, wq_ref, qc_ref, wk_ref, pf_ref, wv_ref, vc_ref, iqk_ref, iall_ref,
                        q_ref, kt_ref, v_ref, st_ref):
    h = _rms(x_ref[...], g_ref[...]).astype(BF16)
    tm = h.shape[0]
    low_half = lax.broadcasted_iota(jnp.int32, (tm, LANES), 1) < DIFF_DH

    def spread(z, fill):
        blocks = []
        for hm in range(2 * DIFF_HEADS):
            pair = z[:, (hm // 2) * LANES:(hm // 2 + 1) * LANES]
            data = pair if hm % 2 == 0 else pltpu.roll(pair, DIFF_DH, axis=1)
            blocks.append(jnp.where(low_half, data, fill(hm)))
        return jnp.concatenate(blocks, axis=1)

    qc = qc_ref[...]
    qa = spread(_dot(h, wq_ref[...]) * (DIFF_DH ** -0.5),
                lambda hm: qc[:, hm * LANES:(hm + 1) * LANES]).astype(BF16)
    q_ref[...] = qa
    pf = pf_ref[...]
    zk = spread(_dot(h, wk_ref[...]), lambda hm: pf)
    kt_ref[...] = zk.T.astype(BF16)
    zv = _dot(h, wv_ref[...]).astype(BF16)
    one = jnp.broadcast_to(vc_ref[...], (zv.shape[0], LANES)).astype(BF16)
    v_ref[...] = jnp.concatenate(
        [part for hd in range(DIFF_HEADS) for part in (zv[:, hd * DIFF_DV:(hd + 1) * DIFF_DV], one)], axis=1)

    qf = qa.astype(F32)
    kf = zk.astype(BF16).astype(F32)

    def block_sums(vals, ind_ref):
        return _dot(vals.astype(BF16), ind_ref[...])

    qn2 = jnp.max(block_sums(qf * qf, iqk_ref), axis=0, keepdims=True)
    kn2 = jnp.max(block_sums(kf * kf, iqk_ref), axis=0, keepdims=True)
    diag = jnp.min(block_sums(qf * kf, iall_ref), axis=0, keepdims=True)
    st_ref[...] = jnp.concatenate([qn2, kn2, diag, jnp.zeros((5, LANES), F32)], axis=0)


def _inproj_diff(x2d, g, wq, qc, wk, pf, wv, vc, iqk, iall, *, batch, seq):
    n, d = x2d.shape
    tm = min(TOKEN_TILE, seq)
    tps = seq // tm
    wq_cols = wk_cols = wv_cols = DIFF_HEADS * 2 * LANES
    return pl.pallas_call(
        _inproj_diff_kernel,
        grid=(n // tm,),
        in_specs=[pl.BlockSpec((tm, d), lambda i: (i, 0)), _const_spec(g.shape),
                  _const_spec(wq.shape), _const_spec(qc.shape), _const_spec(wk.shape),
                  pl.BlockSpec((tm, LANES), lambda i: (i % tps, 0)),
                  _const_spec(wv.shape), _const_spec(vc.shape), _const_spec(iqk.shape),
                  _const_spec(iall.shape)],
        out_specs=(pl.BlockSpec((tm, wq_cols), lambda i: (i, 0)),
                   pl.BlockSpec((None, wk_cols, tm), lambda i: (i // tps, 0, i % tps)),
                   pl.BlockSpec((tm, wv_cols), lambda i: (i, 0)),
                   pl.BlockSpec((None, 8, LANES), lambda i: (i, 0, 0))),
        out_shape=(jax.ShapeDtypeStruct((n, wq_cols), BF16),
                   jax.ShapeDtypeStruct((batch, wk_cols, seq), BF16),
                   jax.ShapeDtypeStruct((n, wv_cols), BF16),
                   jax.ShapeDtypeStruct((n // tm, 8, LANES), F32)),
        compiler_params=_params(("arbitrary",)),
        name="inproj_diff",
    )(x2d, g, wq, qc, wk, pf, wv, vc, iqk, iall)


def _gla_kernel(q_ref, k_ref, la_ref, v_ref, sg_ref, gn_ref, o_ref, st_ref, a_ref, *, tg):
    @pl.when(pl.program_id(1) == 0)
    def _():
        st_ref[...] = jnp.zeros(st_ref.shape, F32)

    la = la_ref[...]
    la_hi, la_lo = _split_bf16(la)
    row = lax.broadcasted_iota(jnp.int32, (tg, tg), 0)
    col = lax.broadcasted_iota(jnp.int32, (tg, tg), 1)
    causal = col <= row

    def prefix(sel):
        sel = jnp.where(sel, 1.0, 0.0).astype(BF16)
        return _dot(sel, la_hi) + _dot(sel, la_lo)

    b = prefix(causal)
    btot = b[tg - 1:tg, :]
    q = q_ref[...]
    k = k_ref[...]
    qe = (q * jnp.exp(b)).astype(BF16)
    kd = (k * jnp.exp(btot - b)).astype(BF16)

    safe = jnp.min(btot) >= -GLA_SAFE_LOG_DECAY

    @pl.when(safe)
    def _():
        ke = (k * jnp.exp(-b)).astype(BF16)
        for hd in range(GLA_HEADS):
            sl = slice(hd * GLA_DK, (hd + 1) * GLA_DK)
            a = _dot_nt(qe[:, sl], ke[:, sl])
            a_ref[hd] = jnp.where(causal, a, 0.0).astype(BF16)

    @pl.when(jnp.logical_not(safe))
    def _():
        qb = q.astype(BF16)
        kb = k.astype(BF16)
        acc = [jnp.where(row == col, _dot_nt(qb[:, hd * GLA_DK:(hd + 1) * GLA_DK],
                                             kb[:, hd * GLA_DK:(hd + 1) * GLA_DK]), 0.0)
               for hd in range(GLA_HEADS)]
        rowv = lax.broadcasted_iota(jnp.int32, (tg, 1), 0)
        half = tg // 2
        while half >= 1:
            shift = int(math.log2(2 * half))
            mid = ((row >> shift) << shift) + (half - 1)
            r = prefix(col <= mid)
            second = (rowv & (2 * half - 1)) >= half
            fac = jnp.exp(jnp.where(second, b - r, r - b))
            ql = jnp.where(second, q * fac, 0.0).astype(BF16)
            kl = jnp.where(second, 0.0, k * fac).astype(BF16)
            same = (row >> shift) == (col >> shift)
            for hd in range(GLA_HEADS):
                sl = slice(hd * GLA_DK, (hd + 1) * GLA_DK)
                acc[hd] = acc[hd] + jnp.where(same, _dot_nt(ql[:, sl], kl[:, sl]), 0.0)
            half //= 2
        for hd in range(GLA_HEADS):
            a_ref[hd] = acc[hd].astype(BF16)

    v = v_ref[...]
    gn = gn_ref[...]
    decay = jnp.exp(btot)
    for hd in range(GLA_HEADS):
        sl = slice(hd * GLA_DK, (hd + 1) * GLA_DK)
        vs = slice(hd * GLA_DV, (hd + 1) * GLA_DV)
        st = st_ref[hd]
        o = _dot(a_ref[hd], v[:, vs]) + _dot_nt(qe[:, sl], st.astype(BF16))
        st_ref[hd] = st * decay[:, sl] + _dot_tn(v[:, vs], kd[:, sl])
        o_ref[:, vs] = (_rms(o, gn) * sg_ref[:, vs].astype(F32)).astype(BF16)


def _gla(gq, gk, la, gv, sg, gn, *, batch, seq):
    n = gq.shape[0]
    tg = min(GLA_TILE, seq)
    tps = seq // tg
    row = lambda w: pl.BlockSpec((tg, w), lambda b, j: (b * tps + j, 0))
    return pl.pallas_call(
        functools.partial(_gla_kernel, tg=tg),
        grid=(batch, tps),
        in_specs=[row(GLA_QK), row(GLA_QK), row(GLA_QK), row(GLA_V), row(GLA_V),
                  pl.BlockSpec(gn.shape, lambda b, j: (0, 0))],
        out_specs=row(GLA_V),
        out_shape=jax.ShapeDtypeStruct((n, GLA_V), BF16),
        scratch_shapes=[pltpu.VMEM((GLA_HEADS, GLA_DV, GLA_DK), F32),
                        pltpu.VMEM((GLA_HEADS, tg, tg), BF16)],
        compiler_params=_params(("arbitrary", "arbitrary")),
        name="gla",
    )(gq, gk, la, gv, sg, gn)


def _attn_kernel(qi_ref, kj_ref, kind_ref, first_ref, q_ref, kt_ref, v_ref, *rest, lam_init, tq, tk, steps):
    n_diag = tq // tk
    if n_diag == 2:
        kt2_ref, v2_ref, lam_ref, gn_ref, o_ref, m_ref, acc_ref = rest
    else:
        lam_ref, gn_ref, o_ref, m_ref, acc_ref = rest
    idx = pl.program_id(0) * steps + pl.program_id(1)
    kind = kind_ref[idx]
    ar = min(ATTN_ROWS, tq)

    @pl.when(first_ref[idx] == 1)
    def _():
        m_ref[...] = jnp.full(m_ref.shape, -jnp.inf, F32)
        acc_ref[...] = jnp.zeros(acc_ref.shape, F32)

    def chain(rb, mp, pieces):
        rows = slice(rb * ar, (rb + 1) * ar)
        q = q_ref[rows, mp * LANES:(mp + 1) * LANES]
        scores = []
        for ktr, _, c_end, rel in pieces:
            s = _dot(q, ktr[mp * LANES:(mp + 1) * LANES, 0:c_end])
            if rel is not None:
                row = lax.broadcasted_iota(jnp.int32, (ar, c_end), 0) + rel
                col = lax.broadcasted_iota(jnp.int32, (ar, c_end), 1)
                s = jnp.where(col <= row, s, -jnp.inf)
            scores.append(s)
        m_old = m_ref[mp, rows, :]
        m_new = m_old
        for s in scores:
            m_new = jnp.maximum(m_new, jnp.max(s, axis=1, keepdims=True))
        acc = jnp.exp(m_old - m_new) * acc_ref[mp, rows, :]
        for s, (_, vr, c_end, _) in zip(scores, pieces):
            acc = acc + _dot(jnp.exp((s - m_new).astype(BF16)), vr[0:c_end, :])
        acc_ref[mp, rows, :] = acc
        m_ref[mp, rows, :] = m_new

    def step(diagonal):
        tiles = [(kt_ref, v_ref), (kt2_ref, v2_ref)][:n_diag] if diagonal else [(kt_ref, v_ref)]
        for rb in range(tq // ar):
            pieces = []
            for d, (ktr, vr) in enumerate(tiles):
                rel = rb * ar - d * tk if diagonal else tk
                if rel + ar <= 0:
                    continue
                if rel >= tk - 1:
                    pieces.append((ktr, vr, tk, None))
                else:
                    pieces.append((ktr, vr, min(tk, -(-(rel + ar) // MXU_COLS) * MXU_COLS), rel))
            for mp in range(2):
                chain(rb, mp, pieces)

    @pl.when(kind == -1)
    def _():
        step(False)

    @pl.when(kind == 0)
    def _():
        step(True)
        lp = lam_ref[...]
        lam = (jnp.exp(jnp.sum(lp[0:1] * lp[1:2], axis=1, keepdims=True))
               - jnp.exp(jnp.sum(lp[2:3] * lp[3:4], axis=1, keepdims=True)) + lam_init)
        a1 = acc_ref[0]
        a2 = acc_ref[1]
        o = (a1[:, :DIFF_DV] / a1[:, DIFF_DV:DIFF_DV + 1]
             - lam * (a2[:, :DIFF_DV] / a2[:, DIFF_DV:DIFF_DV + 1]))
        o_ref[...] = (_rms(o, gn_ref[...]) * (1.0 - lam_init)).astype(BF16)


def _attn_tiles(seq):
    tk = min(ATTN_K_TILE, seq)
    tq = min(ATTN_Q_TILE, seq)
    return tq, tk


def _attn_steps(stats, *, batch, seq):
    tq, tk = _attn_tiles(seq)
    nq, nk, n_diag = seq // tq, seq // tk, tq // tk
    tm = min(TOKEN_TILE, seq)
    assert n_diag in (1, 2)
    pairs = [(i, j) for i in range(nq) for j in range(i * n_diag + 1)]
    qi_s = np.array([p[0] for p in pairs], np.int32)
    kj_s = np.array([p[1] for p in pairs], np.int32)
    kind_s = np.array([0 if j == i * n_diag else -1 for i, j in pairs], np.int32)
    steps = len(pairs)

    nhm = 2 * DIFF_HEADS
    per_tile = stats.reshape(batch, seq // tm, 8, LANES)
    q2 = jnp.max(per_tile[:, :, 0, :nhm].reshape(batch, nq, tq // tm, nhm), axis=2)
    k2 = jnp.max(per_tile[:, :, 1, :nhm].reshape(batch, nk, tk // tm, nhm), axis=2)
    dmin = jnp.min(per_tile[:, :, 2, :nhm].reshape(batch, nq, tq // tm, nhm), axis=2)
    slopes = np.repeat([2.0 ** (-8.0 * (hd + 1.0) / DIFF_HEADS) for hd in range(DIFF_HEADS)], 2)
    last_key = ((kj_s + 1) * tk - 1).astype(np.float32)
    bound = (jnp.sqrt(q2[:, qi_s, :] * k2[:, kj_s, :]) * 1.01
             + jnp.asarray(slopes, F32)[None, None, :] * jnp.asarray(last_key)[None, :, None])
    negligible = bound < dmin[:, qi_s, :] - ATTN_SKIP_MARGIN
    negligible = jnp.all(negligible.reshape(batch, steps, DIFF_HEADS, 2), axis=3)
    active = jnp.logical_or(jnp.asarray(kind_s >= 0)[None, :, None], jnp.logical_not(negligible))
    active = active.transpose(0, 2, 1).reshape(batch * DIFF_HEADS, steps)

    pos = jnp.arange(steps, dtype=jnp.int32)
    dest = jnp.cumsum(active.astype(jnp.int32), axis=1) - 1
    n_act = dest[:, -1:] + 1
    sel = jnp.logical_and(active[:, :, None], dest[:, :, None] == pos[None, None, :])
    pick = lambda tab: jnp.sum(jnp.where(sel, jnp.asarray(tab)[None, :, None], 0), axis=1)
    qi, kj, kind = pick(qi_s), pick(kj_s), pick(kind_s)
    idle = pos[None, :] >= n_act
    hold = lambda a: jnp.where(idle, jnp.take_along_axis(a, n_act - 1, axis=1), a)
    qi, kj = hold(qi), hold(kj)
    kind = jnp.where(idle, -2, kind)
    prev_qi = jnp.concatenate([jnp.full_like(qi[:, :1], -1), qi[:, :-1]], axis=1)
    first = jnp.logical_and(qi != prev_qi, jnp.logical_not(idle)).astype(jnp.int32)
    flat = lambda a: a.reshape(-1).astype(jnp.int32)
    return flat(qi), flat(kj), flat(kind), flat(first), steps


def _attn(dq, dkt, dv, stats, lam_p, gn, *, batch, seq, lam_init):
    tq, tk = _attn_tiles(seq)
    qi, kj, kind, first, steps = _attn_steps(stats, batch=batch, seq=seq)
    dq3 = dq.reshape(batch, seq, dq.shape[1])
    dv3 = dv.reshape(batch, seq, dv.shape[1])
    nh = DIFF_HEADS
    n_diag = tq // tk
    at = lambda g, t: g * steps + t
    last = lambda qi, g, t: n_diag * qi[at(g, t)] + n_diag - 1
    second = [
        pl.BlockSpec((None, 2 * LANES, tk), lambda g, t, qi, kj, kd, ft: (g // nh, g % nh, last(qi, g, t))),
        pl.BlockSpec((None, tk, 2 * LANES), lambda g, t, qi, kj, kd, ft: (g // nh, last(qi, g, t), g % nh)),
    ] if n_diag == 2 else []
    grid_spec = pltpu.PrefetchScalarGridSpec(
        num_scalar_prefetch=4,
        grid=(batch * nh, steps),
        in_specs=[
            pl.BlockSpec((None, tq, 2 * LANES), lambda g, t, qi, kj, kd, ft: (g // nh, qi[at(g, t)], g % nh)),
            pl.BlockSpec((None, 2 * LANES, tk), lambda g, t, qi, kj, kd, ft: (g // nh, g % nh, kj[at(g, t)])),
            pl.BlockSpec((None, tk, 2 * LANES), lambda g, t, qi, kj, kd, ft: (g // nh, kj[at(g, t)], g % nh)),
        ] + second + [
            pl.BlockSpec(lam_p.shape, lambda g, t, qi, kj, kd, ft: (0, 0)),
            pl.BlockSpec(gn.shape, lambda g, t, qi, kj, kd, ft: (0, 0)),
        ],
        out_specs=pl.BlockSpec((None, tq, DIFF_DV),
                               lambda g, t, qi, kj, kd, ft: (g // nh, qi[at(g, t)], g % nh)),
        scratch_shapes=[pltpu.VMEM((2, tq, 1), F32), pltpu.VMEM((2, tq, 2 * LANES), F32)],
    )
    out = pl.pallas_call(
        functools.partial(_attn_kernel, lam_init=lam_init, tq=tq, tk=tk, steps=steps),
        grid_spec=grid_spec,
        out_shape=jax.ShapeDtypeStruct((batch, seq, DIFF_V), BF16),
        compiler_params=_params(("arbitrary", "arbitrary")),
        name="diff_attn",
    )(qi, kj, kind, first, dq3, dkt, dv3, *([dkt, dv3] if n_diag == 2 else []), lam_p, gn)
    return out.reshape(batch * seq, DIFF_V)


def _merge_route_kernel(x_ref, oa_ref, ob_ref, oc_ref, gt_ref, woa_ref, wob_ref, woc_ref, wout_ref,
                        gf_ref, wrh_ref, wrl_ref, br_ref, x1_ref, hx_ref, cnt_ref, *, tm, d):
    i = pl.program_id(0)

    @pl.when(i == 0)
    def _():
        cnt_ref[...] = jnp.zeros(cnt_ref.shape, F32)

    ya = _dot(oa_ref[...], woa_ref[...])
    yb = _dot(ob_ref[...], wob_ref[...])
    yc = _dot(oc_ref[...], woc_ref[...])
    merged = (gt_ref[:, 0:d].astype(F32) * ya + gt_ref[:, d:2 * d].astype(F32) * yb
              + gt_ref[:, 2 * d:3 * d].astype(F32) * yc)
    x1 = x_ref[...] + _dot(merged.astype(BF16), wout_ref[...])
    x1_ref[...] = x1
    h2 = _rms(x1, gf_ref[...])
    hx_ref[:, 0:d] = h2

    h_hi, h_lo = _split_bf16(h2)
    both = _dot(h_hi, wrl_ref[...])
    logits = both[:, :LANES] + both[:, LANES:] + _dot(h_lo, wrh_ref[...]) + br_ref[...]
    lane = lax.broadcasted_iota(jnp.int32, (tm, LANES), 1)
    ninf = -jnp.inf
    glog = jnp.where(lane < N_GROUPS, logits, ninf)
    gmax = jnp.max(glog, axis=1, keepdims=True)
    gi = jnp.min(jnp.where(glog == gmax, lane, LANES), axis=1, keepdims=True)
    gp = 1.0 / jnp.sum(jnp.exp(glog - gmax), axis=1, keepdims=True)
    eid = lane - N_GROUPS
    in_group = (eid >= 0) & (eid < N_EXPERTS) & ((eid >> EPG_SHIFT) == gi)
    w0 = jnp.where(in_group, logits, ninf)
    m1 = jnp.max(w0, axis=1, keepdims=True)
    i1 = jnp.min(jnp.where(w0 == m1, lane, LANES), axis=1, keepdims=True)
    w1 = jnp.where(lane == i1, ninf, w0)
    m2 = jnp.max(w1, axis=1, keepdims=True)
    i2 = jnp.min(jnp.where(w1 == m2, lane, LANES), axis=1, keepdims=True)
    e1 = i1 - N_GROUPS
    e2 = i2 - N_GROUPS
    tt = jnp.exp(m2 - m1)
    p1 = gp / (1.0 + tt)
    p2 = gp * tt / (1.0 + tt)
    comb = jnp.where(lane == e1, p1, 0.0) + jnp.where(lane == e2, p2, 0.0)

    lo = jnp.minimum(e1, e2) & (EXPERTS_PER_GROUP - 1)
    hi = jnp.maximum(e1, e2) & (EXPERTS_PER_GROUP - 1)
    pair = ((lo * (2 * EXPERTS_PER_GROUP - 1 - lo)) >> 1) + (hi - lo - 1)
    cls = gi * N_PAIRS + pair
    onehot = lane == cls
    row = lax.broadcasted_iota(jnp.int32, (tm, tm), 0)
    col = lax.broadcasted_iota(jnp.int32, (tm, tm), 1)
    before = jnp.where(col < row, 1.0, 0.0).astype(BF16)
    seen = _dot(before, jnp.where(onehot, 1.0, 0.0).astype(BF16)) + cnt_ref[0:1, :]
    rank = jnp.sum(jnp.where(onehot, seen, 0.0), axis=1, keepdims=True)
    cnt_ref[0:1, :] = cnt_ref[0:1, :] + jnp.sum(jnp.where(onehot, 1.0, 0.0), axis=0, keepdims=True)

    meta = comb + jnp.where(lane == META_CLS, cls.astype(F32), 0.0) \
        + jnp.where(lane == META_RANK, rank, 0.0)
    hx_ref[:, d:d + LANES] = meta


def _merge_route(x2d, oa, ob, oc, gates, woa, wob, woc, wout, gf, wrh, wrl, br, *, seq):
    n, d = x2d.shape
    tm = min(TOKEN_TILE, seq)
    row = lambda w: pl.BlockSpec((tm, w), lambda i: (i, 0))
    return pl.pallas_call(
        functools.partial(_merge_route_kernel, tm=tm, d=d),
        grid=(n // tm,),
        in_specs=[row(d), row(GLA_V), row(DIFF_V), row(CONV_WIDTH), row(N_BRANCH * d),
                  _const_spec(woa.shape), _const_spec(wob.shape), _const_spec(woc.shape),
                  _const_spec(wout.shape), _const_spec(gf.shape), _const_spec(wrh.shape),
                  _const_spec(wrl.shape), _const_spec(br.shape)],
        out_specs=(row(d), row(d + LANES), pl.BlockSpec((8, LANES), lambda i: (0, 0))),
        out_shape=(jax.ShapeDtypeStruct((n, d), F32), jax.ShapeDtypeStruct((n, d + LANES), F32),
                   jax.ShapeDtypeStruct((8, LANES), F32)),
        compiler_params=_params(("arbitrary",)),
        name="merge_route",
    )(x2d, oa, ob, oc, gates, woa, wob, woc, wout, gf, wrh, wrl, br)


def _expert_kernel(e1_ref, e2_ref, nv_ref, src_ref, dst_ref, hx_ref, wgu1_ref, wgu2_ref, wdn1_ref, wdn2_ref,
                   y_ref, *scratch, te, d, dexp, n_tok):
    nb = EXPERT_BUFS
    xbufs, ybufs, (gsem_ref, ssem_ref) = scratch[:nb], scratch[nb:2 * nb], scratch[2 * nb:]
    j = pl.program_id(0)
    nv = nv_ref[0]

    def gather_one(tile, r, par):
        pltpu.make_async_copy(hx_ref.at[pl.ds(src_ref[tile * te + r], 1)], xbufs[par].at[pl.ds(r, 1)],
                              gsem_ref.at[par]).start()

    def scatter_one(tile, r, par):
        pltpu.make_async_copy(ybufs[par].at[pl.ds(r, 1)], y_ref.at[pl.ds(dst_ref[tile * te + r], 1)],
                              ssem_ref.at[par]).start()

    def wait_gather(par):
        pltpu.make_async_copy(hx_ref.at[pl.ds(0, te)], xbufs[par], gsem_ref.at[par]).wait()

    def wait_scatter(par):
        pltpu.make_async_copy(ybufs[par], y_ref.at[pl.ds(0, te)], ssem_ref.at[par]).wait()

    def expert(par, which):
        e_ref, wgu_ref, wdn_ref = ((e1_ref, wgu1_ref, wdn1_ref), (e2_ref, wgu2_ref, wdn2_ref))[which]
        xb = xbufs[par]
        lane = lax.broadcasted_iota(jnp.int32, (te, LANES), 1)
        gu = _dot(xb[:, 0:d].astype(BF16), wgu_ref[...])
        gate = gu[:, :dexp]
        act = (gate * _sigmoid(gate) * gu[:, dexp:]).astype(BF16)
        wcol = jnp.sum(jnp.where(lane == e_ref[j], xb[:, d:d + LANES], 0.0), axis=1, keepdims=True)
        return wcol * _dot(act, wdn_ref[...])

    def compute(par):
        ybufs[par][...] = expert(par, 0) + expert(par, 1)

    def looped(fn, tile, par):
        def body(r, c):
            fn(tile, r, par)
            return c
        lax.fori_loop(0, te, body, 0)

    steady = jnp.logical_and(j >= nb, j + nb - 1 < nv)
    for par in range(nb):
        mine = (j % nb) == par
        ahead = (par + nb - 1) % nb

        @pl.when(jnp.logical_and(mine, steady))
        def _(par=par, ahead=ahead):
            wait_gather(par)
            wait_scatter(par)

            @pl.when(nv > 0)
            def _():
                ybufs[par][...] = expert(par, 0)
                for r in range(te):
                    gather_one(j + nb - 1, r, ahead)

            @pl.when(nv > 1)
            def _():
                ybufs[par][...] += expert(par, 1)
                for r in range(te):
                    scatter_one(j, r, par)

        @pl.when(jnp.logical_and(mine, jnp.logical_not(steady)))
        def _(par=par, ahead=ahead):
            if par == 0:
                @pl.when(j == 0)
                def _():
                    ybufs[0][...] = jnp.zeros((te, d), F32)
                    for k in range(nb):
                        fill = pltpu.make_async_copy(ybufs[0], y_ref.at[pl.ds(n_tok + k * te, te)],
                                                     ssem_ref.at[0])
                        fill.start()
                        fill.wait()
                    for k in range(nb - 1):
                        @pl.when(k < nv)
                        def _(k=k):
                            looped(gather_one, k, k)

            @pl.when(j < nv)
            def _():
                wait_gather(par)

            @pl.when(jnp.logical_and(j >= nb, j - nb < nv))
            def _():
                wait_scatter(par)

            @pl.when(j < nv)
            def _():
                compute(par)
                looped(scatter_one, j, par)

            @pl.when(j + nb - 1 < nv)
            def _():
                looped(gather_one, j + nb - 1, ahead)


def _experts(hx, wgu, wdn, e1, e2, nvalid, src, dst, *, te, n_tiles):
    n, dx = hx.shape
    d = dx - LANES
    dexp = wdn.shape[1]
    grid_spec = pltpu.PrefetchScalarGridSpec(
        num_scalar_prefetch=5,
        grid=(n_tiles + EXPERT_BUFS,),
        in_specs=[
            pl.BlockSpec(memory_space=pl.ANY),
            pl.BlockSpec((None, d, 2 * dexp), lambda j, e1, e2, nv, src, dst: (e1[j], 0, 0)),
            pl.BlockSpec((None, d, 2 * dexp), lambda j, e1, e2, nv, src, dst: (e2[j], 0, 0)),
            pl.BlockSpec((None, dexp, d), lambda j, e1, e2, nv, src, dst: (e1[j], 0, 0)),
            pl.BlockSpec((None, dexp, d), lambda j, e1, e2, nv, src, dst: (e2[j], 0, 0)),
        ],
        out_specs=pl.BlockSpec(memory_space=pl.ANY),
        scratch_shapes=([pltpu.VMEM((te, dx), F32)] * EXPERT_BUFS + [pltpu.VMEM((te, d), F32)] * EXPERT_BUFS
                        + [pltpu.SemaphoreType.DMA((EXPERT_BUFS,)), pltpu.SemaphoreType.DMA((EXPERT_BUFS,))]),
    )
    return pl.pallas_call(
        functools.partial(_expert_kernel, te=te, d=d, dexp=dexp, n_tok=n),
        grid_spec=grid_spec,
        out_shape=jax.ShapeDtypeStruct((n + EXPERT_BUFS * te, d), F32),
        compiler_params=_params(("arbitrary",)),
        name="experts",
    )(e1, e2, nvalid, src, dst, hx, wgu, wgu, wdn, wdn)


_PAIR_LO = np.array([lo for lo in range(EXPERTS_PER_GROUP) for hi in range(lo + 1, EXPERTS_PER_GROUP)],
                    np.int32)
_PAIR_HI = np.array([hi for lo in range(EXPERTS_PER_GROUP) for hi in range(lo + 1, EXPERTS_PER_GROUP)],
                    np.int32)


def _dispatch_tables(hx, cnt, *, te, n_tiles):
    n = hx.shape[0]
    d = hx.shape[1] - LANES
    cls = hx[:, d + META_CLS].astype(jnp.int32)
    rank = hx[:, d + META_RANK].astype(jnp.int32)
    counts = cnt[0, :N_CLASSES].astype(jnp.int32)
    tiles_c = (counts + te - 1) // te
    tile_end = jnp.cumsum(tiles_c)
    tile_start = tile_end - tiles_c
    start_of = jnp.dot(jax.nn.one_hot(cls, N_CLASSES, dtype=F32), tile_start.astype(F32),
                       precision=lax.Precision.HIGHEST)
    dest = start_of.astype(jnp.int32) * te + rank
    slot = jnp.arange(n_tiles * te, dtype=jnp.int32)
    tok = jnp.full((n_tiles * te,), -1, jnp.int32).at[dest].set(jnp.arange(n, dtype=jnp.int32))
    src = jnp.maximum(tok, 0)
    dst = jnp.where(tok < 0, n + ((slot // te) % EXPERT_BUFS) * te + slot % te, tok)
    tile_ids = jnp.arange(n_tiles + EXPERT_BUFS, dtype=jnp.int32)
    tcls = jnp.minimum(jnp.sum((tile_ids[:, None] >= tile_end[None, :]).astype(jnp.int32), axis=1),
                       N_CLASSES - 1)
    grp = tcls // N_PAIRS
    pr = tcls % N_PAIRS
    e1 = grp * EXPERTS_PER_GROUP + jnp.asarray(_PAIR_LO)[pr]
    e2 = grp * EXPERTS_PER_GROUP + jnp.asarray(_PAIR_HI)[pr]
    return e1.astype(jnp.int32), e2.astype(jnp.int32), tile_end[-1:].astype(jnp.int32), src, dst


def _ple_kernel(x1_ref, y_ref, p_ref, wpg_ref, wple_ref, gfin_ref, o_ref, *, final):
    x2 = x1_ref[...] + y_ref[...]
    gate = _sigmoid(_dot(x2.astype(BF16), wpg_ref[...]))
    x3 = x2 + gate * _dot(p_ref[...].astype(BF16), wple_ref[...])
    o_ref[...] = _rms(x3, gfin_ref[...]) if final else x3


def _ple(x1, y, p2d, wpg, wple, gfin, *, seq, final):
    n, d = x1.shape
    tm = min(WIDE_TOKEN_TILE, seq)
    row = lambda w: pl.BlockSpec((tm, w), lambda i: (i, 0))
    return pl.pallas_call(
        functools.partial(_ple_kernel, final=final),
        grid=(n // tm,),
        in_specs=[row(d), row(d), row(p2d.shape[1]), _const_spec(wpg.shape), _const_spec(wple.shape),
                  _const_spec(gfin.shape)],
        out_specs=row(d),
        out_shape=jax.ShapeDtypeStruct((n, d), F32),
        compiler_params=_params(("arbitrary",)),
        name="ple",
    )(x1, y, p2d, wpg, wple, gfin)


def _pad_cols(w, width):
    return jnp.pad(w, ((0, 0), (0, width - w.shape[1])))


def _diff_constants(seq):
    slopes = [2.0 ** (-8.0 * (hd + 1.0) / DIFF_HEADS) for hd in range(DIFF_HEADS)]
    qc = np.zeros((1, 2 * DIFF_HEADS * LANES), np.float32)
    for hd in range(DIFF_HEADS):
        for mp in range(2):
            for dgt in range(N_POS_DIGITS):
                coef = slopes[hd] * float(POS_BASE ** dgt)
                mant = math.frexp(coef)[0] * 256.0
                assert mant == int(mant), "ALiBi coefficient must be exact in bf16"
                qc[0, (2 * hd + mp) * LANES + DIFF_DH + dgt] = coef
    assert seq <= POS_BASE ** N_POS_DIGITS
    pos = np.arange(seq)
    pf = np.zeros((seq, LANES), np.float32)
    for dgt in range(N_POS_DIGITS):
        pf[:, DIFF_DH + dgt] = (pos // (POS_BASE ** dgt)) % POS_BASE
    vc = np.zeros((1, LANES), np.float32)
    vc[0, 0] = 1.0
    iqk = np.zeros((2 * DIFF_HEADS * LANES, LANES), np.float32)
    iall = np.zeros((2 * DIFF_HEADS * LANES, LANES), np.float32)
    for hm in range(2 * DIFF_HEADS):
        iqk[hm * LANES:hm * LANES + DIFF_DH, hm] = 1.0
        iall[hm * LANES:(hm + 1) * LANES, hm] = 1.0
    return (jnp.asarray(qc), jnp.asarray(pf), jnp.asarray(vc), jnp.asarray(iqk, BF16),
            jnp.asarray(iall, BF16))


def kernel(x, p, g_mix, w_in, w_gla_a2, b_gla_a, g_gla_norm, w_o_gla, diff_lam, g_diff_norm, w_o_diff,
           conv_w, w_o_conv, w_out, g_ffn, w_router_group, b_router_group, w_router_expert,
           b_router_expert, w_expert_gate_up, w_expert_down, w_ple, w_ple_gate, g_final):
    batch, seq, d = x.shape
    depth = w_in.shape[0]
    n = batch * seq
    te = min(EXPERT_TILE, seq)
    n_tiles = n // te + N_CLASSES
    qc, pf, vc, iqk, iall = _diff_constants(seq)

    sizes = (GLA_QK, GLA_QK, GLA_V, GLA_V, GLA_RANK, DIFF_QK, DIFF_QK, DIFF_V,
             CONV_WIDTH, CONV_WIDTH, CONV_WIDTH, N_BRANCH * d)
    offs = np.concatenate([[0], np.cumsum(sizes)]).tolist()
    assert offs[-1] == w_in.shape[2]

    x2d = x.reshape(n, d)
    for layer in range(depth):
        wi = w_in[layer]
        seg = lambda a, b: wi[:, offs[a]:offs[b]]
        wgla = seg(0, 4).astype(BF16)
        wga = _pad_cols(seg(4, 5), LANES).astype(BF16)
        w2 = jnp.pad(w_gla_a2[layer], ((0, LANES - GLA_RANK), (0, 0))).astype(BF16)
        wq, wk, wv = seg(5, 6).astype(BF16), seg(6, 7).astype(BF16), seg(7, 8).astype(BF16)
        wconv = seg(8, 11).astype(BF16)
        wgates = seg(11, 12).astype(BF16)
        lam_init = 0.8 - 0.6 * math.exp(-0.3 * layer)

        gq, gk, la, gv, sg, yc, gates = _inproj_main(
            x2d, g_mix[layer][None, :], wgla, wga, w2, b_gla_a[layer][None, :], wconv,
            conv_w[layer], wgates, seq=seq)
        dq, dkt, dv, stats = _inproj_diff(x2d, g_mix[layer][None, :], wq, qc, wk, pf, wv, vc, iqk, iall,
                                          batch=batch, seq=seq)
        oa = _gla(gq, gk, la, gv, sg, g_gla_norm[layer][None, :], batch=batch, seq=seq)
        ob = _attn(dq, dkt, dv, stats, diff_lam[layer], g_diff_norm[layer][None, :],
                   batch=batch, seq=seq, lam_init=lam_init)

        wr = _pad_cols(jnp.concatenate([w_router_group[layer], w_router_expert[layer]], axis=1), LANES)
        wrh = wr.astype(BF16)
        wrl = jnp.concatenate([wrh, (wr - wrh.astype(F32)).astype(BF16)], axis=1)
        br = _pad_cols(jnp.concatenate([b_router_group[layer], b_router_expert[layer]])[None, :], LANES)
        x1, hx, cnt = _merge_route(
            x2d, oa, ob, yc, gates, w_o_gla[layer].astype(BF16), w_o_diff[layer].astype(BF16),
            w_o_conv[layer].astype(BF16), w_out[layer].astype(BF16), g_ffn[layer][None, :],
            wrh, wrl, br, seq=seq)

        e1, e2, nvalid, src, dst = _dispatch_tables(hx, cnt, te=te, n_tiles=n_tiles)
        y = _experts(hx, w_expert_gate_up[layer].astype(BF16), w_expert_down[layer].astype(BF16),
                     e1, e2, nvalid, src, dst, te=te, n_tiles=n_tiles)

        x2d = _ple(x1, y, p[layer].reshape(n, p.shape[-1]), w_ple_gate[layer].astype(BF16),
                   w_ple[layer].astype(BF16), g_final[None, :], seq=seq,
                   final=(layer == depth - 1))
    return x2d.reshape(batch, seq, d)
```

```python
import functools
import math

import numpy as np
import jax
import jax.numpy as jnp
from jax import lax
from jax.experimental import pallas as pl
from jax.experimental.pallas import tpu as pltpu

F32 = jnp.float32
BF16 = jnp.bfloat16

EPS = 1e-6
GLA_HEADS = 4
GLA_DK = 64
GLA_DV = 128
GLA_RANK = 16
GLA_TAU = 16.0
GLA_QK = GLA_HEADS * GLA_DK
GLA_V = GLA_HEADS * GLA_DV
DIFF_HEADS = 4
DIFF_DH = 64
DIFF_DV = 2 * DIFF_DH
DIFF_QK = DIFF_HEADS * 2 * DIFF_DH
DIFF_V = DIFF_HEADS * DIFF_DV
CONV_WIDTH = 512
CONV_K = 3
N_GROUPS = 4
EXPERTS_PER_GROUP = 4
N_EXPERTS = N_GROUPS * EXPERTS_PER_GROUP
EPG_SHIFT = EXPERTS_PER_GROUP.bit_length() - 1
assert 1 << EPG_SHIFT == EXPERTS_PER_GROUP
N_PAIRS = EXPERTS_PER_GROUP * (EXPERTS_PER_GROUP - 1) // 2
N_CLASSES = N_GROUPS * N_PAIRS
N_BRANCH = 3

LANES = 128
POS_BASE = 128
N_POS_DIGITS = 3
GLA_SAFE_LOG_DECAY = 60.0

TOKEN_TILE = 512
WIDE_TOKEN_TILE = 1024
GLA_TILE = 256
ATTN_Q_TILE = 2048
ATTN_K_TILE = 1024
ATTN_ROWS = 128
MXU_COLS = 256
ATTN_SKIP_MARGIN = 92.0
EXPERT_TILE = 256
EXPERT_BUFS = 3
META_CLS = N_EXPERTS
META_RANK = N_EXPERTS + 1

VMEM_LIMIT = 56 * 1024 * 1024


def _const_spec(shape):
    nd = len(shape)
    return pl.BlockSpec(shape, lambda *_: (0,) * nd, pipeline_mode=pl.Buffered(1))


def _params(sem):
    return pltpu.CompilerParams(dimension_semantics=sem, vmem_limit_bytes=VMEM_LIMIT)


def _rms(xf, g):
    return xf * lax.rsqrt(jnp.mean(xf * xf, axis=-1, keepdims=True) + EPS) * g


def _sigmoid(v):
    return 1.0 / (1.0 + jnp.exp(-v))


def _dot(a, b):
    return jnp.dot(a, b, preferred_element_type=F32)


def _dot_nt(a, b):
    return lax.dot_general(a, b, (((1,), (1,)), ((), ())), preferred_element_type=F32)


def _dot_tn(a, b):
    return lax.dot_general(a, b, (((0,), (0,)), ((), ())), preferred_element_type=F32)


def _split_bf16(v):
    hi = v.astype(BF16)
    lo = (v - hi.astype(F32)).astype(BF16)
    return hi, lo


N_MAIN_IN, N_MAIN_OUT, N_DIFF_IN, N_DIFF_OUT = 7, 7, 8, 4


def _inproj_kernel(x_ref, g_ref, *refs, tiles_per_seq, tm):
    h = _rms(x_ref[...], g_ref[...]).astype(BF16)
    a = N_MAIN_IN
    b = a + N_DIFF_IN
    c = b + N_MAIN_OUT
    d = c + N_DIFF_OUT
    _inproj_main_body(h, *refs[:a], *refs[b:c], refs[d], tiles_per_seq=tiles_per_seq, tm=tm)
    _inproj_diff_body(h, *refs[a:b], *refs[c:d])


def _inproj_main_body(h, wgla_ref, wga_ref, w2_ref, ba_ref, wconv_ref, cw_ref,
                      wgates_ref, gq_ref, gk_ref, la_ref, gv_ref, sg_ref, yc_ref, gates_ref,
                      ubuf_ref, *, tiles_per_seq, tm):
    i = pl.program_id(0)

    z = _dot(h, wgla_ref[...])
    gq_ref[...] = z[:, :GLA_QK] * (GLA_DK ** -0.5)
    gk_ref[...] = z[:, GLA_QK:2 * GLA_QK]
    gv_ref[...] = z[:, 2 * GLA_QK:2 * GLA_QK + GLA_V].astype(BF16)
    gg = z[:, 2 * GLA_QK + GLA_V:]
    sg_ref[...] = (gg * _sigmoid(gg)).astype(BF16)

    ga = _dot(h, wga_ref[...])
    xa = _dot(ga.astype(BF16), w2_ref[...]) + ba_ref[...]
    log_sig = jnp.minimum(xa, 0.0) - jnp.log(1.0 + jnp.exp(-jnp.abs(xa)))
    la_ref[...] = log_sig * (1.0 / GLA_TAU)

    zc = _dot(h, wconv_ref[...])
    u = zc[:, 2 * CONV_WIDTH:] * zc[:, :CONV_WIDTH]

    @pl.when(i % tiles_per_seq == 0)
    def _():
        ubuf_ref[0:8, :] = jnp.zeros((8, CONV_WIDTH), F32)

    ubuf_ref[8:8 + tm, :] = u
    cw = cw_ref[...]
    y = (cw[0:1, :] * ubuf_ref[6:6 + tm, :] + cw[1:2, :] * ubuf_ref[7:7 + tm, :]
         + cw[2:3, :] * u)
    yc_ref[...] = (zc[:, CONV_WIDTH:2 * CONV_WIDTH] * y).astype(BF16)
    ubuf_ref[0:8, :] = ubuf_ref[tm:tm + 8, :]

    gates_ref[...] = _sigmoid(_dot(h, wgates_ref[...])).astype(BF16)


def _inproj(x2d, g, main_w, diff_w, *, batch, seq):
    n, d = x2d.shape
    tm = min(TOKEN_TILE, seq)
    tps = seq // tm
    assert len(main_w) == N_MAIN_IN and len(diff_w) == N_DIFF_IN
    row = lambda w: pl.BlockSpec((tm, w), lambda i: (i, 0))
    seq_major = lambda rows: pl.BlockSpec((None, rows, tm), lambda i: (i // tps, 0, i % tps))
    aug = DIFF_HEADS * 2 * LANES
    diff_specs = [_const_spec(w.shape) for w in diff_w]
    diff_specs[3] = pl.BlockSpec((tm, LANES), lambda i: (i % tps, 0))
    out_shape = (
        jax.ShapeDtypeStruct((n, GLA_QK), F32), jax.ShapeDtypeStruct((n, GLA_QK), F32),
        jax.ShapeDtypeStruct((n, GLA_QK), F32), jax.ShapeDtypeStruct((n, GLA_V), BF16),
        jax.ShapeDtypeStruct((n, GLA_V), BF16), jax.ShapeDtypeStruct((n, CONV_WIDTH), BF16),
        jax.ShapeDtypeStruct((n, N_BRANCH * d), BF16),
        jax.ShapeDtypeStruct((n, aug), BF16), jax.ShapeDtypeStruct((batch, aug, seq), BF16),
        jax.ShapeDtypeStruct((n, aug), BF16), jax.ShapeDtypeStruct((n // tm, 8, LANES), F32))
    return pl.pallas_call(
        functools.partial(_inproj_kernel, tiles_per_seq=tps, tm=tm),
        grid=(n // tm,),
        in_specs=[row(d), _const_spec(g.shape)] + [_const_spec(w.shape) for w in main_w] + diff_specs,
        out_specs=(row(GLA_QK), row(GLA_QK), row(GLA_QK), row(GLA_V), row(GLA_V),
                   row(CONV_WIDTH), row(N_BRANCH * d),
                   row(aug), seq_major(aug), row(aug), pl.BlockSpec((None, 8, LANES), lambda i: (i, 0, 0))),
        out_shape=out_shape,
        scratch_shapes=[pltpu.VMEM((8 + tm, CONV_WIDTH), F32)],
        compiler_params=_params(("arbitrary",)),
        name="inproj",
    )(x2d, g, *main_w, *diff_w)


def _inproj_diff_body(h, wq_ref, qc_ref, wk_ref, pf_ref, wv_ref, vc_ref, iqk_ref, iall_ref,
                      q_ref, kt_ref, v_ref, st_ref):
    tm = h.shape[0]
    low_half = lax.broadcasted_iota(jnp.int32, (tm, LANES), 1) < DIFF_DH

    def spread(z, fill):
        blocks = []
        for hm in range(2 * DIFF_HEADS):
            pair = z[:, (hm // 2) * LANES:(hm // 2 + 1) * LANES]
            data = pair if hm % 2 == 0 else pltpu.roll(pair, DIFF_DH, axis=1)
            blocks.append(jnp.where(low_half, data, fill(hm)))
        return jnp.concatenate(blocks, axis=1)

    qc = qc_ref[...]
    qa = spread(_dot(h, wq_ref[...]) * (DIFF_DH ** -0.5),
                lambda hm: qc[:, hm * LANES:(hm + 1) * LANES]).astype(BF16)
    q_ref[...] = qa
    pf = pf_ref[...]
    zk = spread(_dot(h, wk_ref[...]), lambda hm: pf)
    kt_ref[...] = zk.T.astype(BF16)
    zv = _dot(h, wv_ref[...]).astype(BF16)
    one = jnp.broadcast_to(vc_ref[...], (zv.shape[0], LANES)).astype(BF16)
    v_ref[...] = jnp.concatenate(
        [part for hd in range(DIFF_HEADS) for part in (zv[:, hd * DIFF_DV:(hd + 1) * DIFF_DV], one)], axis=1)

    qf = qa.astype(F32)
    kf = zk.astype(BF16).astype(F32)

    def block_sums(vals, ind_ref):
        return _dot(vals.astype(BF16), ind_ref[...])

    qn2 = jnp.max(block_sums(qf * qf, iqk_ref), axis=0, keepdims=True)
    kn2 = jnp.max(block_sums(kf * kf, iqk_ref), axis=0, keepdims=True)
    diag = jnp.min(block_sums(qf * kf, iall_ref), axis=0, keepdims=True)
    st_ref[...] = jnp.concatenate([qn2, kn2, diag, jnp.zeros((5, LANES), F32)], axis=0)


def _gla_kernel(q_ref, k_ref, la_ref, v_ref, sg_ref, gn_ref, o_ref, st_ref, a_ref, *, tg):
    @pl.when(pl.program_id(1) == 0)
    def _():
        st_ref[...] = jnp.zeros(st_ref.shape, F32)

    la = la_ref[...]
    la_hi, la_lo = _split_bf16(la)
    row = lax.broadcasted_iota(jnp.int32, (tg, tg), 0)
    col = lax.broadcasted_iota(jnp.int32, (tg, tg), 1)
    causal = col <= row

    def prefix(sel):
        sel = jnp.where(sel, 1.0, 0.0).astype(BF16)
        return _dot(sel, la_hi) + _dot(sel, la_lo)

    b = prefix(causal)
    btot = b[tg - 1:tg, :]
    q = q_ref[...]
    k = k_ref[...]
    qe = (q * jnp.exp(b)).astype(BF16)
    kd = (k * jnp.exp(btot - b)).astype(BF16)

    safe = jnp.min(btot) >= -GLA_SAFE_LOG_DECAY

    @pl.when(safe)
    def _():
        ke = (k * jnp.exp(-b)).astype(BF16)
        for hd in range(GLA_HEADS):
            sl = slice(hd * GLA_DK, (hd + 1) * GLA_DK)
            a = _dot_nt(qe[:, sl], ke[:, sl])
            a_ref[hd] = jnp.where(causal, a, 0.0).astype(BF16)

    @pl.when(jnp.logical_not(safe))
    def _():
        qb = q.astype(BF16)
        kb = k.astype(BF16)
        acc = [jnp.where(row == col, _dot_nt(qb[:, hd * GLA_DK:(hd + 1) * GLA_DK],
                                             kb[:, hd * GLA_DK:(hd + 1) * GLA_DK]), 0.0)
               for hd in range(GLA_HEADS)]
        rowv = lax.broadcasted_iota(jnp.int32, (tg, 1), 0)
        half = tg // 2
        while half >= 1:
            shift = int(math.log2(2 * half))
            mid = ((row >> shift) << shift) + (half - 1)
            r = prefix(col <= mid)
            second = (rowv & (2 * half - 1)) >= half
            fac = jnp.exp(jnp.where(second, b - r, r - b))
            ql = jnp.where(second, q * fac, 0.0).astype(BF16)
            kl = jnp.where(second, 0.0, k * fac).astype(BF16)
            same = (row >> shift) == (col >> shift)
            for hd in range(GLA_HEADS):
                sl = slice(hd * GLA_DK, (hd + 1) * GLA_DK)
                acc[hd] = acc[hd] + jnp.where(same, _dot_nt(ql[:, sl], kl[:, sl]), 0.0)
            half //= 2
        for hd in range(GLA_HEADS):
            a_ref[hd] = acc[hd].astype(BF16)

    v = v_ref[...]
    gn = gn_ref[...]
    decay = jnp.exp(btot)
    for hd in range(GLA_HEADS):
        sl = slice(hd * GLA_DK, (hd + 1) * GLA_DK)
        vs = slice(hd * GLA_DV, (hd + 1) * GLA_DV)
        st = st_ref[hd]
        o = _dot(a_ref[hd], v[:, vs]) + _dot_nt(qe[:, sl], st.astype(BF16))
        st_ref[hd] = st * decay[:, sl] + _dot_tn(v[:, vs], kd[:, sl])
        o_ref[:, vs] = (_rms(o, gn) * sg_ref[:, vs].astype(F32)).astype(BF16)


def _gla(gq, gk, la, gv, sg, gn, *, batch, seq):
    n = gq.shape[0]
    tg = min(GLA_TILE, seq)
    tps = seq // tg
    row = lambda w: pl.BlockSpec((tg, w), lambda b, j: (b * tps + j, 0))
    return pl.pallas_call(
        functools.partial(_gla_kernel, tg=tg),
        grid=(batch, tps),
        in_specs=[row(GLA_QK), row(GLA_QK), row(GLA_QK), row(GLA_V), row(GLA_V),
                  pl.BlockSpec(gn.shape, lambda b, j: (0, 0))],
        out_specs=row(GLA_V),
        out_shape=jax.ShapeDtypeStruct((n, GLA_V), BF16),
        scratch_shapes=[pltpu.VMEM((GLA_HEADS, GLA_DV, GLA_DK), F32),
                        pltpu.VMEM((GLA_HEADS, tg, tg), BF16)],
        compiler_params=_params(("arbitrary", "arbitrary")),
        name="gla",
    )(gq, gk, la, gv, sg, gn)


def _attn_kernel(qi_ref, kj_ref, kind_ref, first_ref, q_ref, kt_ref, v_ref, *rest, lam_init, tq, tk, steps):
    n_diag = tq // tk
    if n_diag == 2:
        kt2_ref, v2_ref, lam_ref, gn_ref, o_ref, m_ref, acc_ref = rest
    else:
        lam_ref, gn_ref, o_ref, m_ref, acc_ref = rest
    idx = pl.program_id(0) * steps + pl.program_id(1)
    kind = kind_ref[idx]
    ar = min(ATTN_ROWS, tq)

    @pl.when(first_ref[idx] == 1)
    def _():
        m_ref[...] = jnp.full(m_ref.shape, -jnp.inf, F32)
        acc_ref[...] = jnp.zeros(acc_ref.shape, F32)

    def chain(rb, mp, pieces):
        rows = slice(rb * ar, (rb + 1) * ar)
        q = q_ref[rows, mp * LANES:(mp + 1) * LANES]
        scores = []
        for ktr, _, c_end, rel in pieces:
            s = _dot(q, ktr[mp * LANES:(mp + 1) * LANES, 0:c_end])
            if rel is not None:
                row = lax.broadcasted_iota(jnp.int32, (ar, c_end), 0) + rel
                col = lax.broadcasted_iota(jnp.int32, (ar, c_end), 1)
                s = jnp.where(col <= row, s, -jnp.inf)
            scores.append(s)
        m_old = m_ref[mp, rows, :]
        m_new = m_old
        for s in scores:
            m_new = jnp.maximum(m_new, jnp.max(s, axis=1, keepdims=True))
        acc = jnp.exp(m_old - m_new) * acc_ref[mp, rows, :]
        for s, (_, vr, c_end, _) in zip(scores, pieces):
            acc = acc + _dot(jnp.exp((s - m_new).astype(BF16)), vr[0:c_end, :])
        acc_ref[mp, rows, :] = acc
        m_ref[mp, rows, :] = m_new

    def step(diagonal):
        tiles = [(kt_ref, v_ref), (kt2_ref, v2_ref)][:n_diag] if diagonal else [(kt_ref, v_ref)]
        for rb in range(tq // ar):
            pieces = []
            for d, (ktr, vr) in enumerate(tiles):
                rel = rb * ar - d * tk if diagonal else tk
                if rel + ar <= 0:
                    continue
                if rel >= tk - 1:
                    pieces.append((ktr, vr, tk, None))
                else:
                    pieces.append((ktr, vr, min(tk, -(-(rel + ar) // MXU_COLS) * MXU_COLS), rel))
            for mp in range(2):
                chain(rb, mp, pieces)

    @pl.when(kind == -1)
    def _():
        step(False)

    @pl.when(kind == 0)
    def _():
        step(True)
        lp = lam_ref[...]
        lam = (jnp.exp(jnp.sum(lp[0:1] * lp[1:2], axis=1, keepdims=True))
               - jnp.exp(jnp.sum(lp[2:3] * lp[3:4], axis=1, keepdims=True)) + lam_init)
        a1 = acc_ref[0]
        a2 = acc_ref[1]
        o = (a1[:, :DIFF_DV] / a1[:, DIFF_DV:DIFF_DV + 1]
             - lam * (a2[:, :DIFF_DV] / a2[:, DIFF_DV:DIFF_DV + 1]))
        o_ref[...] = (_rms(o, gn_ref[...]) * (1.0 - lam_init)).astype(BF16)


def _attn_tiles(seq):
    tk = min(ATTN_K_TILE, seq)
    tq = min(ATTN_Q_TILE, seq)
    return tq, tk


def _attn_steps(stats, *, batch, seq):
    tq, tk = _attn_tiles(seq)
    nq, nk, n_diag = seq // tq, seq // tk, tq // tk
    tm = min(TOKEN_TILE, seq)
    assert n_diag in (1, 2)
    pairs = [(i, j) for i in range(nq) for j in range(i * n_diag + 1)]
    qi_s = np.array([p[0] for p in pairs], np.int32)
    kj_s = np.array([p[1] for p in pairs], np.int32)
    kind_s = np.array([0 if j == i * n_diag else -1 for i, j in pairs], np.int32)
    steps = len(pairs)

    nhm = 2 * DIFF_HEADS
    per_tile = stats.reshape(batch, seq // tm, 8, LANES)
    q2 = jnp.max(per_tile[:, :, 0, :nhm].reshape(batch, nq, tq // tm, nhm), axis=2)
    k2 = jnp.max(per_tile[:, :, 1, :nhm].reshape(batch, nk, tk // tm, nhm), axis=2)
    dmin = jnp.min(per_tile[:, :, 2, :nhm].reshape(batch, nq, tq // tm, nhm), axis=2)
    slopes = np.repeat([2.0 ** (-8.0 * (hd + 1.0) / DIFF_HEADS) for hd in range(DIFF_HEADS)], 2)
    last_key = ((kj_s + 1) * tk - 1).astype(np.float32)
    bound = (jnp.sqrt(q2[:, qi_s, :] * k2[:, kj_s, :]) * 1.01
             + jnp.asarray(slopes, F32)[None, None, :] * jnp.asarray(last_key)[None, :, None])
    negligible = bound < dmin[:, qi_s, :] - ATTN_SKIP_MARGIN
    negligible = jnp.all(negligible.reshape(batch, steps, DIFF_HEADS, 2), axis=3)
    active = jnp.logical_or(jnp.asarray(kind_s >= 0)[None, :, None], jnp.logical_not(negligible))
    active = active.transpose(0, 2, 1).reshape(batch * DIFF_HEADS, steps)

    pos = jnp.arange(steps, dtype=jnp.int32)
    dest = jnp.cumsum(active.astype(jnp.int32), axis=1) - 1
    n_act = dest[:, -1:] + 1
    sel = jnp.logical_and(active[:, :, None], dest[:, :, None] == pos[None, None, :])
    pick = lambda tab: jnp.sum(jnp.where(sel, jnp.asarray(tab)[None, :, None], 0), axis=1)
    qi, kj, kind = pick(qi_s), pick(kj_s), pick(kind_s)
    idle = pos[None, :] >= n_act
    hold = lambda a: jnp.where(idle, jnp.take_along_axis(a, n_act - 1, axis=1), a)
    qi, kj = hold(qi), hold(kj)
    kind = jnp.where(idle, -2, kind)
    prev_qi = jnp.concatenate([jnp.full_like(qi[:, :1], -1), qi[:, :-1]], axis=1)
    first = jnp.logical_and(qi != prev_qi, jnp.logical_not(idle)).astype(jnp.int32)
    flat = lambda a: a.reshape(-1).astype(jnp.int32)
    return flat(qi), flat(kj), flat(kind), flat(first), steps


def _attn(dq, dkt, dv, stats, lam_p, gn, *, batch, seq, lam_init):
    tq, tk = _attn_tiles(seq)
    qi, kj, kind, first, steps = _attn_steps(stats, batch=batch, seq=seq)
    dq3 = dq.reshape(batch, seq, dq.shape[1])
    dv3 = dv.reshape(batch, seq, dv.shape[1])
    nh = DIFF_HEADS
    n_diag = tq // tk
    at = lambda g, t: g * steps + t
    last = lambda qi, g, t: n_diag * qi[at(g, t)] + n_diag - 1
    second = [
        pl.BlockSpec((None, 2 * LANES, tk), lambda g, t, qi, kj, kd, ft: (g // nh, g % nh, last(qi, g, t))),
        pl.BlockSpec((None, tk, 2 * LANES), lambda g, t, qi, kj, kd, ft: (g // nh, last(qi, g, t), g % nh)),
    ] if n_diag == 2 else []
    grid_spec = pltpu.PrefetchScalarGridSpec(
        num_scalar_prefetch=4,
        grid=(batch * nh, steps),
        in_specs=[
            pl.BlockSpec((None, tq, 2 * LANES), lambda g, t, qi, kj, kd, ft: (g // nh, qi[at(g, t)], g % nh)),
            pl.BlockSpec((None, 2 * LANES, tk), lambda g, t, qi, kj, kd, ft: (g // nh, g % nh, kj[at(g, t)])),
            pl.BlockSpec((None, tk, 2 * LANES), lambda g, t, qi, kj, kd, ft: (g // nh, kj[at(g, t)], g % nh)),
        ] + second + [
            pl.BlockSpec(lam_p.shape, lambda g, t, qi, kj, kd, ft: (0, 0)),
            pl.BlockSpec(gn.shape, lambda g, t, qi, kj, kd, ft: (0, 0)),
        ],
        out_specs=pl.BlockSpec((None, tq, DIFF_DV),
                               lambda g, t, qi, kj, kd, ft: (g // nh, qi[at(g, t)], g % nh)),
        scratch_shapes=[pltpu.VMEM((2, tq, 1), F32), pltpu.VMEM((2, tq, 2 * LANES), F32)],
    )
    out = pl.pallas_call(
        functools.partial(_attn_kernel, lam_init=lam_init, tq=tq, tk=tk, steps=steps),
        grid_spec=grid_spec,
        out_shape=jax.ShapeDtypeStruct((batch, seq, DIFF_V), BF16),
        compiler_params=_params(("arbitrary", "arbitrary")),
        name="diff_attn",
    )(qi, kj, kind, first, dq3, dkt, dv3, *([dkt, dv3] if n_diag == 2 else []), lam_p, gn)
    return out.reshape(batch * seq, DIFF_V)


def _merge_route_kernel(x_ref, oa_ref, ob_ref, oc_ref, gt_ref, woa_ref, wob_ref, woc_ref, wout_ref,
                        gf_ref, wrh_ref, wrl_ref, br_ref, x1_ref, hx_ref, cnt_ref, *, tm, d):
    i = pl.program_id(0)

    @pl.when(i == 0)
    def _():
        cnt_ref[...] = jnp.zeros(cnt_ref.shape, F32)

    ya = _dot(oa_ref[...], woa_ref[...])
    yb = _dot(ob_ref[...], wob_ref[...])
    yc = _dot(oc_ref[...], woc_ref[...])
    merged = (gt_ref[:, 0:d].astype(F32) * ya + gt_ref[:, d:2 * d].astype(F32) * yb
              + gt_ref[:, 2 * d:3 * d].astype(F32) * yc)
    x1 = x_ref[...] + _dot(merged.astype(BF16), wout_ref[...])
    x1_ref[...] = x1
    h2 = _rms(x1, gf_ref[...])
    hx_ref[:, 0:d] = h2

    h_hi, h_lo = _split_bf16(h2)
    both = _dot(h_hi, wrl_ref[...])
    logits = both[:, :LANES] + both[:, LANES:] + _dot(h_lo, wrh_ref[...]) + br_ref[...]
    lane = lax.broadcasted_iota(jnp.int32, (tm, LANES), 1)
    ninf = -jnp.inf
    glog = jnp.where(lane < N_GROUPS, logits, ninf)
    gmax = jnp.max(glog, axis=1, keepdims=True)
    gi = jnp.min(jnp.where(glog == gmax, lane, LANES), axis=1, keepdims=True)
    gp = 1.0 / jnp.sum(jnp.exp(glog - gmax), axis=1, keepdims=True)
    eid = lane - N_GROUPS
    in_group = (eid >= 0) & (eid < N_EXPERTS) & ((eid >> EPG_SHIFT) == gi)
    w0 = jnp.where(in_group, logits, ninf)
    m1 = jnp.max(w0, axis=1, keepdims=True)
    i1 = jnp.min(jnp.where(w0 == m1, lane, LANES), axis=1, keepdims=True)
    w1 = jnp.where(lane == i1, ninf, w0)
    m2 = jnp.max(w1, axis=1, keepdims=True)
    i2 = jnp.min(jnp.where(w1 == m2, lane, LANES), axis=1, keepdims=True)
    e1 = i1 - N_GROUPS
    e2 = i2 - N_GROUPS
    tt = jnp.exp(m2 - m1)
    p1 = gp / (1.0 + tt)
    p2 = gp * tt / (1.0 + tt)
    comb = jnp.where(lane == e1, p1, 0.0) + jnp.where(lane == e2, p2, 0.0)

    lo = jnp.minimum(e1, e2) & (EXPERTS_PER_GROUP - 1)
    hi = jnp.maximum(e1, e2) & (EXPERTS_PER_GROUP - 1)
    pair = ((lo * (2 * EXPERTS_PER_GROUP - 1 - lo)) >> 1) + (hi - lo - 1)
    cls = gi * N_PAIRS + pair
    onehot = lane == cls
    row = lax.broadcasted_iota(jnp.int32, (tm, tm), 0)
    col = lax.broadcasted_iota(jnp.int32, (tm, tm), 1)
    before = jnp.where(col < row, 1.0, 0.0).astype(BF16)
    seen = _dot(before, jnp.where(onehot, 1.0, 0.0).astype(BF16)) + cnt_ref[0:1, :]
    rank = jnp.sum(jnp.where(onehot, seen, 0.0), axis=1, keepdims=True)
    cnt_ref[0:1, :] = cnt_ref[0:1, :] + jnp.sum(jnp.where(onehot, 1.0, 0.0), axis=0, keepdims=True)

    meta = comb + jnp.where(lane == META_CLS, cls.astype(F32), 0.0) \
        + jnp.where(lane == META_RANK, rank, 0.0)
    hx_ref[:, d:d + LANES] = meta


def _merge_route(x2d, oa, ob, oc, gates, woa, wob, woc, wout, gf, wrh, wrl, br, *, seq):
    n, d = x2d.shape
    tm = min(TOKEN_TILE, seq)
    row = lambda w: pl.BlockSpec((tm, w), lambda i: (i, 0))
    return pl.pallas_call(
        functools.partial(_merge_route_kernel, tm=tm, d=d),
        grid=(n // tm,),
        in_specs=[row(d), row(GLA_V), row(DIFF_V), row(CONV_WIDTH), row(N_BRANCH * d),
                  _const_spec(woa.shape), _const_spec(wob.shape), _const_spec(woc.shape),
                  _const_spec(wout.shape), _const_spec(gf.shape), _const_spec(wrh.shape),
                  _const_spec(wrl.shape), _const_spec(br.shape)],
        out_specs=(row(d), row(d + LANES), pl.BlockSpec((8, LANES), lambda i: (0, 0))),
        out_shape=(jax.ShapeDtypeStruct((n, d), F32), jax.ShapeDtypeStruct((n, d + LANES), F32),
                   jax.ShapeDtypeStruct((8, LANES), F32)),
        compiler_params=_params(("arbitrary",)),
        name="merge_route",
    )(x2d, oa, ob, oc, gates, woa, wob, woc, wout, gf, wrh, wrl, br)


def _expert_kernel(e1_ref, e2_ref, nv_ref, src_ref, dst_ref, hx_ref, wgu1_ref, wgu2_ref, wdn1_ref, wdn2_ref,
                   y_ref, *scratch, te, d, dexp, n_tok):
    nb = EXPERT_BUFS
    xbufs, ybufs, (gsem_ref, ssem_ref) = scratch[:nb], scratch[nb:2 * nb], scratch[2 * nb:]
    j = pl.program_id(0)
    nv = nv_ref[0]

    def gather_one(tile, r, par):
        pltpu.make_async_copy(hx_ref.at[pl.ds(src_ref[tile * te + r], 1)], xbufs[par].at[pl.ds(r, 1)],
                              gsem_ref.at[par]).start()

    def scatter_one(tile, r, par):
        pltpu.make_async_copy(ybufs[par].at[pl.ds(r, 1)], y_ref.at[pl.ds(dst_ref[tile * te + r], 1)],
                              ssem_ref.at[par]).start()

    def wait_gather(par):
        pltpu.make_async_copy(hx_ref.at[pl.ds(0, te)], xbufs[par], gsem_ref.at[par]).wait()

    def wait_scatter(par):
        pltpu.make_async_copy(ybufs[par], y_ref.at[pl.ds(0, te)], ssem_ref.at[par]).wait()

    def expert(par, which):
        e_ref, wgu_ref, wdn_ref = ((e1_ref, wgu1_ref, wdn1_ref), (e2_ref, wgu2_ref, wdn2_ref))[which]
        xb = xbufs[par]
        lane = lax.broadcasted_iota(jnp.int32, (te, LANES), 1)
        gu = _dot(xb[:, 0:d].astype(BF16), wgu_ref[...])
        gate = gu[:, :dexp]
        act = (gate * _sigmoid(gate) * gu[:, dexp:]).astype(BF16)
        wcol = jnp.sum(jnp.where(lane == e_ref[j], xb[:, d:d + LANES], 0.0), axis=1, keepdims=True)
        return wcol * _dot(act, wdn_ref[...])

    def compute(par):
        ybufs[par][...] = expert(par, 0) + expert(par, 1)

    def looped(fn, tile, par):
        def body(r, c):
            fn(tile, r, par)
            return c
        lax.fori_loop(0, te, body, 0)

    steady = jnp.logical_and(j >= nb, j + nb - 1 < nv)
    for par in range(nb):
        mine = (j % nb) == par
        ahead = (par + nb - 1) % nb

        @pl.when(jnp.logical_and(mine, steady))
        def _(par=par, ahead=ahead):
            wait_gather(par)
            wait_scatter(par)

            @pl.when(nv > 0)
            def _():
                ybufs[par][...] = expert(par, 0)
                for r in range(te):
                    gather_one(j + nb - 1, r, ahead)

            @pl.when(nv > 1)
            def _():
                ybufs[par][...] += expert(par, 1)
                for r in range(te):
                    scatter_one(j, r, par)

        @pl.when(jnp.logical_and(mine, jnp.logical_not(steady)))
        def _(par=par, ahead=ahead):
            if par == 0:
                @pl.when(j == 0)
                def _():
                    ybufs[0][...] = jnp.zeros((te, d), F32)
                    for k in range(nb):
                        fill = pltpu.make_async_copy(ybufs[0], y_ref.at[pl.ds(n_tok + k * te, te)],
                                                     ssem_ref.at[0])
                        fill.start()
                        fill.wait()
                    for k in range(nb - 1):
                        @pl.when(k < nv)
                        def _(k=k):
                            looped(gather_one, k, k)

            @pl.when(j < nv)
            def _():
                wait_gather(par)

            @pl.when(jnp.logical_and(j >= nb, j - nb < nv))
            def _():
                wait_scatter(par)

            @pl.when(j < nv)
            def _():
                compute(par)
                looped(scatter_one, j, par)

            @pl.when(j + nb - 1 < nv)
            def _():
                looped(gather_one, j + nb - 1, ahead)


def _experts(hx, wgu, wdn, e1, e2, nvalid, src, dst, *, te, n_tiles):
    n, dx = hx.shape
    d = dx - LANES
    dexp = wdn.shape[1]
    grid_spec = pltpu.PrefetchScalarGridSpec(
        num_scalar_prefetch=5,
        grid=(n_tiles + EXPERT_BUFS,),
        in_specs=[
            pl.BlockSpec(memory_space=pl.ANY),
            pl.BlockSpec((None, d, 2 * dexp), lambda j, e1, e2, nv, src, dst: (e1[j], 0, 0)),
            pl.BlockSpec((None, d, 2 * dexp), lambda j, e1, e2, nv, src, dst: (e2[j], 0, 0)),
            pl.BlockSpec((None, dexp, d), lambda j, e1, e2, nv, src, dst: (e1[j], 0, 0)),
            pl.BlockSpec((None, dexp, d), lambda j, e1, e2, nv, src, dst: (e2[j], 0, 0)),
        ],
        out_specs=pl.BlockSpec(memory_space=pl.ANY),
        scratch_shapes=([pltpu.VMEM((te, dx), F32)] * EXPERT_BUFS + [pltpu.VMEM((te, d), F32)] * EXPERT_BUFS
                        + [pltpu.SemaphoreType.DMA((EXPERT_BUFS,)), pltpu.SemaphoreType.DMA((EXPERT_BUFS,))]),
    )
    return pl.pallas_call(
        functools.partial(_expert_kernel, te=te, d=d, dexp=dexp, n_tok=n),
        grid_spec=grid_spec,
        out_shape=jax.ShapeDtypeStruct((n + EXPERT_BUFS * te, d), F32),
        compiler_params=_params(("arbitrary",)),
        name="experts",
    )(e1, e2, nvalid, src, dst, hx, wgu, wgu, wdn, wdn)


_PAIR_LO = np.array([lo for lo in range(EXPERTS_PER_GROUP) for hi in range(lo + 1, EXPERTS_PER_GROUP)],
                    np.int32)
_PAIR_HI = np.array([hi for lo in range(EXPERTS_PER_GROUP) for hi in range(lo + 1, EXPERTS_PER_GROUP)],
                    np.int32)


def _dispatch_tables(hx, cnt, *, te, n_tiles):
    n = hx.shape[0]
    d = hx.shape[1] - LANES
    cls = hx[:, d + META_CLS].astype(jnp.int32)
    rank = hx[:, d + META_RANK].astype(jnp.int32)
    counts = cnt[0, :N_CLASSES].astype(jnp.int32)
    tiles_c = (counts + te - 1) // te
    tile_end = jnp.cumsum(tiles_c)
    tile_start = tile_end - tiles_c
    start_of = jnp.dot(jax.nn.one_hot(cls, N_CLASSES, dtype=F32), tile_start.astype(F32),
                       precision=lax.Precision.HIGHEST)
    dest = start_of.astype(jnp.int32) * te + rank
    slot = jnp.arange(n_tiles * te, dtype=jnp.int32)
    tok = jnp.full((n_tiles * te,), -1, jnp.int32).at[dest].set(jnp.arange(n, dtype=jnp.int32))
    src = jnp.maximum(tok, 0)
    dst = jnp.where(tok < 0, n + ((slot // te) % EXPERT_BUFS) * te + slot % te, tok)
    tile_ids = jnp.arange(n_tiles + EXPERT_BUFS, dtype=jnp.int32)
    tcls = jnp.minimum(jnp.sum((tile_ids[:, None] >= tile_end[None, :]).astype(jnp.int32), axis=1),
                       N_CLASSES - 1)
    grp = tcls // N_PAIRS
    pr = tcls % N_PAIRS
    e1 = grp * EXPERTS_PER_GROUP + jnp.asarray(_PAIR_LO)[pr]
    e2 = grp * EXPERTS_PER_GROUP + jnp.asarray(_PAIR_HI)[pr]
    return e1.astype(jnp.int32), e2.astype(jnp.int32), tile_end[-1:].astype(jnp.int32), src, dst


def _ple_kernel(x1_ref, y_ref, p_ref, wpg_ref, wple_ref, gfin_ref, o_ref, *, final):
    x2 = x1_ref[...] + y_ref[...]
    gate = _sigmoid(_dot(x2.astype(BF16), wpg_ref[...]))
    x3 = x2 + gate * _dot(p_ref[...].astype(BF16), wple_ref[...])
    o_ref[...] = _rms(x3, gfin_ref[...]) if final else x3


def _ple(x1, y, p2d, wpg, wple, gfin, *, seq, final):
    n, d = x1.shape
    tm = min(WIDE_TOKEN_TILE, seq)
    row = lambda w: pl.BlockSpec((tm, w), lambda i: (i, 0))
    return pl.pallas_call(
        functools.partial(_ple_kernel, final=final),
        grid=(n // tm,),
        in_specs=[row(d), row(d), row(p2d.shape[1]), _const_spec(wpg.shape), _const_spec(wple.shape),
                  _const_spec(gfin.shape)],
        out_specs=row(d),
        out_shape=jax.ShapeDtypeStruct((n, d), F32),
        compiler_params=_params(("arbitrary",)),
        name="ple",
    )(x1, y, p2d, wpg, wple, gfin)


def _pad_cols(w, width):
    return jnp.pad(w, ((0, 0), (0, width - w.shape[1])))


def _diff_constants(seq):
    slopes = [2.0 ** (-8.0 * (hd + 1.0) / DIFF_HEADS) for hd in range(DIFF_HEADS)]
    qc = np.zeros((1, 2 * DIFF_HEADS * LANES), np.float32)
    for hd in range(DIFF_HEADS):
        for mp in range(2):
            for dgt in range(N_POS_DIGITS):
                coef = slopes[hd] * float(POS_BASE ** dgt)
                mant = math.frexp(coef)[0] * 256.0
                assert mant == int(mant), "ALiBi coefficient must be exact in bf16"
                qc[0, (2 * hd + mp) * LANES + DIFF_DH + dgt] = coef
    assert seq <= POS_BASE ** N_POS_DIGITS
    pos = np.arange(seq)
    pf = np.zeros((seq, LANES), np.float32)
    for dgt in range(N_POS_DIGITS):
        pf[:, DIFF_DH + dgt] = (pos // (POS_BASE ** dgt)) % POS_BASE
    vc = np.zeros((1, LANES), np.float32)
    vc[0, 0] = 1.0
    iqk = np.zeros((2 * DIFF_HEADS * LANES, LANES), np.float32)
    iall = np.zeros((2 * DIFF_HEADS * LANES, LANES), np.float32)
    for hm in range(2 * DIFF_HEADS):
        iqk[hm * LANES:hm * LANES + DIFF_DH, hm] = 1.0
        iall[hm * LANES:(hm + 1) * LANES, hm] = 1.0
    return (jnp.asarray(qc), jnp.asarray(pf), jnp.asarray(vc), jnp.asarray(iqk, BF16),
            jnp.asarray(iall, BF16))


def kernel(x, p, g_mix, w_in, w_gla_a2, b_gla_a, g_gla_norm, w_o_gla, diff_lam, g_diff_norm, w_o_diff,
           conv_w, w_o_conv, w_out, g_ffn, w_router_group, b_router_group, w_router_expert,
           b_router_expert, w_expert_gate_up, w_expert_down, w_ple, w_ple_gate, g_final):
    batch, seq, d = x.shape
    depth = w_in.shape[0]
    n = batch * seq
    te = min(EXPERT_TILE, seq)
    n_tiles = n // te + N_CLASSES
    qc, pf, vc, iqk, iall = _diff_constants(seq)

    sizes = (GLA_QK, GLA_QK, GLA_V, GLA_V, GLA_RANK, DIFF_QK, DIFF_QK, DIFF_V,
             CONV_WIDTH, CONV_WIDTH, CONV_WIDTH, N_BRANCH * d)
    offs = np.concatenate([[0], np.cumsum(sizes)]).tolist()
    assert offs[-1] == w_in.shape[2]

    x2d = x.reshape(n, d)
    for layer in range(depth):
        wi = w_in[layer]
        seg = lambda a, b: wi[:, offs[a]:offs[b]]
        wgla = seg(0, 4).astype(BF16)
        wga = _pad_cols(seg(4, 5), LANES).astype(BF16)
        w2 = jnp.pad(w_gla_a2[layer], ((0, LANES - GLA_RANK), (0, 0))).astype(BF16)
        wq, wk, wv = seg(5, 6).astype(BF16), seg(6, 7).astype(BF16), seg(7, 8).astype(BF16)
        wconv = seg(8, 11).astype(BF16)
        wgates = seg(11, 12).astype(BF16)
        lam_init = 0.8 - 0.6 * math.exp(-0.3 * layer)

        gq, gk, la, gv, sg, yc, gates, dq, dkt, dv, stats = _inproj(
            x2d, g_mix[layer][None, :],
            (wgla, wga, w2, b_gla_a[layer][None, :], wconv, conv_w[layer], wgates),
            (wq, qc, wk, pf, wv, vc, iqk, iall), batch=batch, seq=seq)
        oa = _gla(gq, gk, la, gv, sg, g_gla_norm[layer][None, :], batch=batch, seq=seq)
        ob = _attn(dq, dkt, dv, stats, diff_lam[layer], g_diff_norm[layer][None, :],
                   batch=batch, seq=seq, lam_init=lam_init)

        wr = _pad_cols(jnp.concatenate([w_router_group[layer], w_router_expert[layer]], axis=1), LANES)
        wrh = wr.astype(BF16)
        wrl = jnp.concatenate([wrh, (wr - wrh.astype(F32)).astype(BF16)], axis=1)
        br = _pad_cols(jnp.concatenate([b_router_group[layer], b_router_expert[layer]])[None, :], LANES)
        x1, hx, cnt = _merge_route(
            x2d, oa, ob, yc, gates, w_o_gla[layer].astype(BF16), w_o_diff[layer].astype(BF16),
            w_o_conv[layer].astype(BF16), w_out[layer].astype(BF16), g_ffn[layer][None, :],
            wrh, wrl, br, seq=seq)

        e1, e2, nvalid, src, dst = _dispatch_tables(hx, cnt, te=te, n_tiles=n_tiles)
        y = _experts(hx, w_expert_gate_up[layer].astype(BF16), w_expert_down[layer].astype(BF16),
                     e1, e2, nvalid, src, dst, te=te, n_tiles=n_tiles)

        x2d = _ple(x1, y, p[layer].reshape(n, p.shape[-1]), w_ple_gate[layer].astype(BF16),
                   w_ple[layer].astype(BF16), g_final[None, :], seq=seq,
                   final=(layer == depth - 1))
    return x2d.reshape(batch, seq, d)
```

```python
import functools
import math

import numpy as np
import jax
import jax.numpy as jnp
from jax import lax
from jax.experimental import pallas as pl
from jax.experimental.pallas import tpu as pltpu

F32 = jnp.float32
BF16 = jnp.bfloat16

EPS = 1e-6
GLA_HEADS = 4
GLA_DK = 64
GLA_DV = 128
GLA_RANK = 16
GLA_TAU = 16.0
GLA_QK = GLA_HEADS * GLA_DK
GLA_V = GLA_HEADS * GLA_DV
DIFF_HEADS = 4
DIFF_DH = 64
DIFF_DV = 2 * DIFF_DH
DIFF_QK = DIFF_HEADS * 2 * DIFF_DH
DIFF_V = DIFF_HEADS * DIFF_DV
CONV_WIDTH = 512
CONV_K = 3
N_GROUPS = 4
EXPERTS_PER_GROUP = 4
N_EXPERTS = N_GROUPS * EXPERTS_PER_GROUP
EPG_SHIFT = EXPERTS_PER_GROUP.bit_length() - 1
assert 1 << EPG_SHIFT == EXPERTS_PER_GROUP
N_PAIRS = EXPERTS_PER_GROUP * (EXPERTS_PER_GROUP - 1) // 2
N_CLASSES = N_GROUPS * N_PAIRS
N_BRANCH = 3

LANES = 128
POS_BASE = 128
N_POS_DIGITS = 3
GLA_SAFE_LOG_DECAY = 60.0

TOKEN_TILE = 512
WIDE_TOKEN_TILE = 1024
GLA_TILE = 256
ATTN_Q_TILE = 2048
ATTN_K_TILE = 1024
ATTN_ROWS = 128
MXU_COLS = 256
ATTN_SKIP_MARGIN = 92.0
EXPERT_TILE = 256
EXPERT_BUFS = 3
META_CLS = N_EXPERTS
META_RANK = N_EXPERTS + 1

VMEM_LIMIT = 56 * 1024 * 1024


def _const_spec(shape):
    nd = len(shape)
    return pl.BlockSpec(shape, lambda *_: (0,) * nd, pipeline_mode=pl.Buffered(1))


def _params(sem):
    return pltpu.CompilerParams(dimension_semantics=sem, vmem_limit_bytes=VMEM_LIMIT)


def _rms(xf, g):
    return xf * lax.rsqrt(jnp.mean(xf * xf, axis=-1, keepdims=True) + EPS) * g


def _sigmoid(v):
    return 1.0 / (1.0 + jnp.exp(-v))


def _dot(a, b):
    return jnp.dot(a, b, preferred_element_type=F32)


def _dot_nt(a, b):
    return lax.dot_general(a, b, (((1,), (1,)), ((), ())), preferred_element_type=F32)


def _dot_tn(a, b):
    return lax.dot_general(a, b, (((0,), (0,)), ((), ())), preferred_element_type=F32)


def _split_bf16(v):
    hi = v.astype(BF16)
    lo = (v - hi.astype(F32)).astype(BF16)
    return hi, lo


N_MAIN_IN, N_MAIN_OUT, N_DIFF_IN, N_DIFF_OUT = 7, 7, 8, 4


def _inproj_kernel(x_ref, g_ref, *refs, tiles_per_seq, tm):
    h = _rms(x_ref[...], g_ref[...]).astype(BF16)
    a = N_MAIN_IN
    b = a + N_DIFF_IN
    c = b + N_MAIN_OUT
    d = c + N_DIFF_OUT
    _inproj_main_body(h, *refs[:a], *refs[b:c], refs[d], tiles_per_seq=tiles_per_seq, tm=tm)
    _inproj_diff_body(h, *refs[a:b], *refs[c:d])


def _inproj_main_body(h, wgla_ref, wga_ref, w2_ref, ba_ref, wconv_ref, cw_ref,
                      wgates_ref, gq_ref, gk_ref, la_ref, gv_ref, sg_ref, yc_ref, gates_ref,
                      ubuf_ref, *, tiles_per_seq, tm):
    i = pl.program_id(0)

    z = _dot(h, wgla_ref[...])
    gq_ref[...] = z[:, :GLA_QK] * (GLA_DK ** -0.5)
    gk_ref[...] = z[:, GLA_QK:2 * GLA_QK]
    gv_ref[...] = z[:, 2 * GLA_QK:2 * GLA_QK + GLA_V].astype(BF16)
    gg = z[:, 2 * GLA_QK + GLA_V:]
    sg_ref[...] = (gg * _sigmoid(gg)).astype(BF16)

    ga = _dot(h, wga_ref[...])
    xa = _dot(ga.astype(BF16), w2_ref[...]) + ba_ref[...]
    log_sig = jnp.minimum(xa, 0.0) - jnp.log(1.0 + jnp.exp(-jnp.abs(xa)))
    la_ref[...] = log_sig * (1.0 / GLA_TAU)

    zc = _dot(h, wconv_ref[...])
    u = zc[:, 2 * CONV_WIDTH:] * zc[:, :CONV_WIDTH]

    @pl.when(i % tiles_per_seq == 0)
    def _():
        ubuf_ref[0:8, :] = jnp.zeros((8, CONV_WIDTH), F32)

    ubuf_ref[8:8 + tm, :] = u
    cw = cw_ref[...]
    y = (cw[0:1, :] * ubuf_ref[6:6 + tm, :] + cw[1:2, :] * ubuf_ref[7:7 + tm, :]
         + cw[2:3, :] * u)
    yc_ref[...] = (zc[:, CONV_WIDTH:2 * CONV_WIDTH] * y).astype(BF16)
    ubuf_ref[0:8, :] = ubuf_ref[tm:tm + 8, :]

    gates_ref[...] = _sigmoid(_dot(h, wgates_ref[...])).astype(BF16)


def _inproj(x2d, g, main_w, diff_w, *, batch, seq):
    n, d = x2d.shape
    tm = min(TOKEN_TILE, seq)
    tps = seq // tm
    assert len(main_w) == N_MAIN_IN and len(diff_w) == N_DIFF_IN
    row = lambda w: pl.BlockSpec((tm, w), lambda i: (i, 0))
    seq_major = lambda rows: pl.BlockSpec((None, rows, tm), lambda i: (i // tps, 0, i % tps))
    aug = DIFF_HEADS * 2 * LANES
    diff_specs = [_const_spec(w.shape) for w in diff_w]
    diff_specs[3] = pl.BlockSpec((tm, LANES), lambda i: (i % tps, 0))
    out_shape = (
        jax.ShapeDtypeStruct((n, GLA_QK), F32), jax.ShapeDtypeStruct((n, GLA_QK), F32),
        jax.ShapeDtypeStruct((n, GLA_QK), F32), jax.ShapeDtypeStruct((n, GLA_V), BF16),
        jax.ShapeDtypeStruct((n, GLA_V), BF16), jax.ShapeDtypeStruct((n, CONV_WIDTH), BF16),
        jax.ShapeDtypeStruct((n, N_BRANCH * d), BF16),
        jax.ShapeDtypeStruct((n, aug), BF16), jax.ShapeDtypeStruct((batch, aug, seq), BF16),
        jax.ShapeDtypeStruct((n, aug), BF16), jax.ShapeDtypeStruct((n // tm, 8, LANES), F32))
    return pl.pallas_call(
        functools.partial(_inproj_kernel, tiles_per_seq=tps, tm=tm),
        grid=(n // tm,),
        in_specs=[row(d), _const_spec(g.shape)] + [_const_spec(w.shape) for w in main_w] + diff_specs,
        out_specs=(row(GLA_QK), row(GLA_QK), row(GLA_QK), row(GLA_V), row(GLA_V),
                   row(CONV_WIDTH), row(N_BRANCH * d),
                   row(aug), seq_major(aug), row(aug), pl.BlockSpec((None, 8, LANES), lambda i: (i, 0, 0))),
        out_shape=out_shape,
        scratch_shapes=[pltpu.VMEM((8 + tm, CONV_WIDTH), F32)],
        compiler_params=_params(("arbitrary",)),
        name="inproj",
    )(x2d, g, *main_w, *diff_w)


def _inproj_diff_body(h, wq_ref, qc_ref, wk_ref, pf_ref, wv_ref, vc_ref, iqk_ref, iall_ref,
                      q_ref, kt_ref, v_ref, st_ref):
    tm = h.shape[0]
    low_half = lax.broadcasted_iota(jnp.int32, (tm, LANES), 1) < DIFF_DH

    def spread(z, fill):
        blocks = []
        for hm in range(2 * DIFF_HEADS):
            pair = z[:, (hm // 2) * LANES:(hm // 2 + 1) * LANES]
            data = pair if hm % 2 == 0 else pltpu.roll(pair, DIFF_DH, axis=1)
            blocks.append(jnp.where(low_half, data, fill(hm)))
        return jnp.concatenate(blocks, axis=1)

    qc = qc_ref[...]
    qa = spread(_dot(h, wq_ref[...]) * (DIFF_DH ** -0.5),
                lambda hm: qc[:, hm * LANES:(hm + 1) * LANES]).astype(BF16)
    q_ref[...] = qa
    pf = pf_ref[...]
    zk = spread(_dot(h, wk_ref[...]), lambda hm: pf)
    kt_ref[...] = zk.T.astype(BF16)
    zv = _dot(h, wv_ref[...]).astype(BF16)
    one = jnp.broadcast_to(vc_ref[...], (zv.shape[0], LANES)).astype(BF16)
    v_ref[...] = jnp.concatenate(
        [part for hd in range(DIFF_HEADS) for part in (zv[:, hd * DIFF_DV:(hd + 1) * DIFF_DV], one)], axis=1)

    qf = qa.astype(F32)
    kf = zk.astype(BF16).astype(F32)

    def block_sums(vals, ind_ref):
        return _dot(vals.astype(BF16), ind_ref[...])

    qn2 = jnp.max(block_sums(qf * qf, iqk_ref), axis=0, keepdims=True)
    kn2 = jnp.max(block_sums(kf * kf, iqk_ref), axis=0, keepdims=True)
    diag = jnp.min(block_sums(qf * kf, iall_ref), axis=0, keepdims=True)
    st_ref[...] = jnp.concatenate([qn2, kn2, diag, jnp.zeros((5, LANES), F32)], axis=0)


def _gla_kernel(q_ref, k_ref, la_ref, v_ref, sg_ref, gn_ref, o_ref, st_ref, a_ref, *, tg):
    @pl.when(pl.program_id(1) == 0)
    def _():
        st_ref[...] = jnp.zeros(st_ref.shape, F32)

    la = la_ref[...]
    la_hi, la_lo = _split_bf16(la)
    row = lax.broadcasted_iota(jnp.int32, (tg, tg), 0)
    col = lax.broadcasted_iota(jnp.int32, (tg, tg), 1)
    causal = col <= row

    def prefix(sel):
        sel = jnp.where(sel, 1.0, 0.0).astype(BF16)
        return _dot(sel, la_hi) + _dot(sel, la_lo)

    b = prefix(causal)
    btot = b[tg - 1:tg, :]
    q = q_ref[...]
    k = k_ref[...]
    qe = (q * jnp.exp(b)).astype(BF16)
    kd = (k * jnp.exp(btot - b)).astype(BF16)

    safe = jnp.min(btot) >= -GLA_SAFE_LOG_DECAY

    @pl.when(safe)
    def _():
        ke = (k * jnp.exp(-b)).astype(BF16)
        for hd in range(GLA_HEADS):
            sl = slice(hd * GLA_DK, (hd + 1) * GLA_DK)
            a = _dot_nt(qe[:, sl], ke[:, sl])
            a_ref[hd] = jnp.where(causal, a, 0.0).astype(BF16)

    @pl.when(jnp.logical_not(safe))
    def _():
        qb = q.astype(BF16)
        kb = k.astype(BF16)
        acc = [jnp.where(row == col, _dot_nt(qb[:, hd * GLA_DK:(hd + 1) * GLA_DK],
                                             kb[:, hd * GLA_DK:(hd + 1) * GLA_DK]), 0.0)
               for hd in range(GLA_HEADS)]
        rowv = lax.broadcasted_iota(jnp.int32, (tg, 1), 0)
        half = tg // 2
        while half >= 1:
            shift = int(math.log2(2 * half))
            mid = ((row >> shift) << shift) + (half - 1)
            r = prefix(col <= mid)
            second = (rowv & (2 * half - 1)) >= half
            fac = jnp.exp(jnp.where(second, b - r, r - b))
            ql = jnp.where(second, q * fac, 0.0).astype(BF16)
            kl = jnp.where(second, 0.0, k * fac).astype(BF16)
            same = (row >> shift) == (col >> shift)
            for hd in range(GLA_HEADS):
                sl = slice(hd * GLA_DK, (hd + 1) * GLA_DK)
                acc[hd] = acc[hd] + jnp.where(same, _dot_nt(ql[:, sl], kl[:, sl]), 0.0)
            half //= 2
        for hd in range(GLA_HEADS):
            a_ref[hd] = acc[hd].astype(BF16)

    v = v_ref[...]
    gn = gn_ref[...]
    decay = jnp.exp(btot)
    for hd in range(GLA_HEADS):
        sl = slice(hd * GLA_DK, (hd + 1) * GLA_DK)
        vs = slice(hd * GLA_DV, (hd + 1) * GLA_DV)
        st = st_ref[hd]
        o = _dot(a_ref[hd], v[:, vs]) + _dot_nt(qe[:, sl], st.astype(BF16))
        st_ref[hd] = st * decay[:, sl] + _dot_tn(v[:, vs], kd[:, sl])
        o_ref[:, vs] = (_rms(o, gn) * sg_ref[:, vs].astype(F32)).astype(BF16)


def _gla(gq, gk, la, gv, sg, gn, *, batch, seq):
    n = gq.shape[0]
    tg = min(GLA_TILE, seq)
    tps = seq // tg
    row = lambda w: pl.BlockSpec((tg, w), lambda b, j: (b * tps + j, 0))
    return pl.pallas_call(
        functools.partial(_gla_kernel, tg=tg),
        grid=(batch, tps),
        in_specs=[row(GLA_QK), row(GLA_QK), row(GLA_QK), row(GLA_V), row(GLA_V),
                  pl.BlockSpec(gn.shape, lambda b, j: (0, 0))],
        out_specs=row(GLA_V),
        out_shape=jax.ShapeDtypeStruct((n, GLA_V), BF16),
        scratch_shapes=[pltpu.VMEM((GLA_HEADS, GLA_DV, GLA_DK), F32),
                        pltpu.VMEM((GLA_HEADS, tg, tg), BF16)],
        compiler_params=_params(("arbitrary", "arbitrary")),
        name="gla",
    )(gq, gk, la, gv, sg, gn)


def _attn_kernel(qi_ref, kj_ref, kind_ref, first_ref, q_ref, kt_ref, v_ref, *rest, lam_init, tq, tk, steps):
    n_diag = tq // tk
    if n_diag == 2:
        kt2_ref, v2_ref, lam_ref, gn_ref, o_ref, m_ref, acc_ref = rest
    else:
        lam_ref, gn_ref, o_ref, m_ref, acc_ref = rest
    idx = pl.program_id(0) * steps + pl.program_id(1)
    kind = kind_ref[idx]
    ar = min(ATTN_ROWS, tq)

    @pl.when(first_ref[idx] == 1)
    def _():
        m_ref[...] = jnp.full(m_ref.shape, -jnp.inf, F32)
        acc_ref[...] = jnp.zeros(acc_ref.shape, F32)

    def chain(rb, mp, pieces):
        rows = slice(rb * ar, (rb + 1) * ar)
        q = q_ref[rows, mp * LANES:(mp + 1) * LANES]
        scores = []
        for ktr, _, c_end, rel in pieces:
            s = _dot(q, ktr[mp * LANES:(mp + 1) * LANES, 0:c_end])
            if rel is not None:
                row = lax.broadcasted_iota(jnp.int32, (ar, c_end), 0) + rel
                col = lax.broadcasted_iota(jnp.int32, (ar, c_end), 1)
                s = jnp.where(col <= row, s, -jnp.inf)
            scores.append(s)
        m_old = m_ref[mp, rows, :]
        m_new = m_old
        for s in scores:
            m_new = jnp.maximum(m_new, jnp.max(s, axis=1, keepdims=True))
        acc = jnp.exp(m_old - m_new) * acc_ref[mp, rows, :]
        for s, (_, vr, c_end, _) in zip(scores, pieces):
            acc = acc + _dot(jnp.exp((s - m_new).astype(BF16)), vr[0:c_end, :])
        acc_ref[mp, rows, :] = acc
        m_ref[mp, rows, :] = m_new

    def step(diagonal):
        tiles = [(kt_ref, v_ref), (kt2_ref, v2_ref)][:n_diag] if diagonal else [(kt_ref, v_ref)]
        for rb in range(tq // ar):
            pieces = []
            for d, (ktr, vr) in enumerate(tiles):
                rel = rb * ar - d * tk if diagonal else tk
                if rel + ar <= 0:
                    continue
                if rel >= tk - 1:
                    pieces.append((ktr, vr, tk, None))
                else:
                    pieces.append((ktr, vr, min(tk, -(-(rel + ar) // MXU_COLS) * MXU_COLS), rel))
            for mp in range(2):
                chain(rb, mp, pieces)

    @pl.when(kind == -1)
    def _():
        step(False)

    @pl.when(kind == 0)
    def _():
        step(True)
        lp = lam_ref[...]
        lam = (jnp.exp(jnp.sum(lp[0:1] * lp[1:2], axis=1, keepdims=True))
               - jnp.exp(jnp.sum(lp[2:3] * lp[3:4], axis=1, keepdims=True)) + lam_init)
        a1 = acc_ref[0]
        a2 = acc_ref[1]
        o = (a1[:, :DIFF_DV] / a1[:, DIFF_DV:DIFF_DV + 1]
             - lam * (a2[:, :DIFF_DV] / a2[:, DIFF_DV:DIFF_DV + 1]))
        o_ref[...] = (_rms(o, gn_ref[...]) * (1.0 - lam_init)).astype(BF16)


def _attn_tiles(seq):
    tk = min(ATTN_K_TILE, seq)
    tq = min(ATTN_Q_TILE, seq)
    return tq, tk


def _attn_steps(stats, *, batch, seq):
    tq, tk = _attn_tiles(seq)
    nq, nk, n_diag = seq // tq, seq // tk, tq // tk
    tm = min(TOKEN_TILE, seq)
    assert n_diag in (1, 2)
    pairs = [(i, j) for i in range(nq) for j in range(i * n_diag + 1)]
    qi_s = np.array([p[0] for p in pairs], np.int32)
    kj_s = np.array([p[1] for p in pairs], np.int32)
    kind_s = np.array([0 if j == i * n_diag else -1 for i, j in pairs], np.int32)
    steps = len(pairs)

    nhm = 2 * DIFF_HEADS
    per_tile = stats.reshape(batch, seq // tm, 8, LANES)
    q2 = jnp.max(per_tile[:, :, 0, :nhm].reshape(batch, nq, tq // tm, nhm), axis=2)
    k2 = jnp.max(per_tile[:, :, 1, :nhm].reshape(batch, nk, tk // tm, nhm), axis=2)
    dmin = jnp.min(per_tile[:, :, 2, :nhm].reshape(batch, nq, tq // tm, nhm), axis=2)
    slopes = np.repeat([2.0 ** (-8.0 * (hd + 1.0) / DIFF_HEADS) for hd in range(DIFF_HEADS)], 2)
    last_key = ((kj_s + 1) * tk - 1).astype(np.float32)
    bound = (jnp.sqrt(q2[:, qi_s, :] * k2[:, kj_s, :]) * 1.01
             + jnp.asarray(slopes, F32)[None, None, :] * jnp.asarray(last_key)[None, :, None])
    negligible = bound < dmin[:, qi_s, :] - ATTN_SKIP_MARGIN
    negligible = jnp.all(negligible.reshape(batch, steps, DIFF_HEADS, 2), axis=3)
    active = jnp.logical_or(jnp.asarray(kind_s >= 0)[None, :, None], jnp.logical_not(negligible))
    active = active.transpose(0, 2, 1).reshape(batch * DIFF_HEADS, steps)

    pos = jnp.arange(steps, dtype=jnp.int32)
    dest = jnp.cumsum(active.astype(jnp.int32), axis=1) - 1
    n_act = dest[:, -1:] + 1
    sel = jnp.logical_and(active[:, :, None], dest[:, :, None] == pos[None, None, :])
    pick = lambda tab: jnp.sum(jnp.where(sel, jnp.asarray(tab)[None, :, None], 0), axis=1)
    qi, kj, kind = pick(qi_s), pick(kj_s), pick(kind_s)
    idle = pos[None, :] >= n_act
    hold = lambda a: jnp.where(idle, jnp.take_along_axis(a, n_act - 1, axis=1), a)
    qi, kj = hold(qi), hold(kj)
    kind = jnp.where(idle, -2, kind)
    prev_qi = jnp.concatenate([jnp.full_like(qi[:, :1], -1), qi[:, :-1]], axis=1)
    first = jnp.logical_and(qi != prev_qi, jnp.logical_not(idle)).astype(jnp.int32)
    flat = lambda a: a.reshape(-1).astype(jnp.int32)
    return flat(qi), flat(kj), flat(kind), flat(first), steps


def _attn(dq, dkt, dv, stats, lam_p, gn, *, batch, seq, lam_init):
    tq, tk = _attn_tiles(seq)
    qi, kj, kind, first, steps = _attn_steps(stats, batch=batch, seq=seq)
    dq3 = dq.reshape(batch, seq, dq.shape[1])
    dv3 = dv.reshape(batch, seq, dv.shape[1])
    nh = DIFF_HEADS
    n_diag = tq // tk
    at = lambda g, t: g * steps + t
    last = lambda qi, g, t: n_diag * qi[at(g, t)] + n_diag - 1
    second = [
        pl.BlockSpec((None, 2 * LANES, tk), lambda g, t, qi, kj, kd, ft: (g // nh, g % nh, last(qi, g, t))),
        pl.BlockSpec((None, tk, 2 * LANES), lambda g, t, qi, kj, kd, ft: (g // nh, last(qi, g, t), g % nh)),
    ] if n_diag == 2 else []
    grid_spec = pltpu.PrefetchScalarGridSpec(
        num_scalar_prefetch=4,
        grid=(batch * nh, steps),
        in_specs=[
            pl.BlockSpec((None, tq, 2 * LANES), lambda g, t, qi, kj, kd, ft: (g // nh, qi[at(g, t)], g % nh)),
            pl.BlockSpec((None, 2 * LANES, tk), lambda g, t, qi, kj, kd, ft: (g // nh, g % nh, kj[at(g, t)])),
            pl.BlockSpec((None, tk, 2 * LANES), lambda g, t, qi, kj, kd, ft: (g // nh, kj[at(g, t)], g % nh)),
        ] + second + [
            pl.BlockSpec(lam_p.shape, lambda g, t, qi, kj, kd, ft: (0, 0)),
            pl.BlockSpec(gn.shape, lambda g, t, qi, kj, kd, ft: (0, 0)),
        ],
        out_specs=pl.BlockSpec((None, tq, DIFF_DV),
                               lambda g, t, qi, kj, kd, ft: (g // nh, qi[at(g, t)], g % nh)),
        scratch_shapes=[pltpu.VMEM((2, tq, 1), F32), pltpu.VMEM((2, tq, 2 * LANES), F32)],
    )
    out = pl.pallas_call(
        functools.partial(_attn_kernel, lam_init=lam_init, tq=tq, tk=tk, steps=steps),
        grid_spec=grid_spec,
        out_shape=jax.ShapeDtypeStruct((batch, seq, DIFF_V), BF16),
        compiler_params=_params(("arbitrary", "arbitrary")),
        name="diff_attn",
    )(qi, kj, kind, first, dq3, dkt, dv3, *([dkt, dv3] if n_diag == 2 else []), lam_p, gn)
    return out.reshape(batch * seq, DIFF_V)


def _merge_route_kernel(x_ref, oa_ref, ob_ref, oc_ref, gt_ref, woa_ref, wob_ref, woc_ref, wout_ref,
                        gf_ref, wrh_ref, wrl_ref, br_ref, x1_ref, hx_ref, cnt_ref, *, tm, d):
    i = pl.program_id(0)

    @pl.when(i == 0)
    def _():
        cnt_ref[...] = jnp.zeros(cnt_ref.shape, F32)

    ya = _dot(oa_ref[...], woa_ref[...])
    yb = _dot(ob_ref[...], wob_ref[...])
    yc = _dot(oc_ref[...], woc_ref[...])
    merged = (gt_ref[:, 0:d].astype(F32) * ya + gt_ref[:, d:2 * d].astype(F32) * yb
              + gt_ref[:, 2 * d:3 * d].astype(F32) * yc)
    x1 = x_ref[...] + _dot(merged.astype(BF16), wout_ref[...])
    x1_ref[...] = x1
    h2 = _rms(x1, gf_ref[...])
    hx_ref[:, 0:d] = h2

    h_hi, h_lo = _split_bf16(h2)
    both = _dot(h_hi, wrl_ref[...])
    logits = both[:, :LANES] + both[:, LANES:] + _dot(h_lo, wrh_ref[...]) + br_ref[...]
    lane = lax.broadcasted_iota(jnp.int32, (tm, LANES), 1)
    ninf = -jnp.inf
    glog = jnp.where(lane < N_GROUPS, logits, ninf)
    gmax = jnp.max(glog, axis=1, keepdims=True)
    gi = jnp.min(jnp.where(glog == gmax, lane, LANES), axis=1, keepdims=True)
    gp = 1.0 / jnp.sum(jnp.exp(glog - gmax), axis=1, keepdims=True)
    eid = lane - N_GROUPS
    in_group = (eid >= 0) & (eid < N_EXPERTS) & ((eid >> EPG_SHIFT) == gi)
    w0 = jnp.where(in_group, logits, ninf)
    m1 = jnp.max(w0, axis=1, keepdims=True)
    i1 = jnp.min(jnp.where(w0 == m1, lane, LANES), axis=1, keepdims=True)
    w1 = jnp.where(lane == i1, ninf, w0)
    m2 = jnp.max(w1, axis=1, keepdims=True)
    i2 = jnp.min(jnp.where(w1 == m2, lane, LANES), axis=1, keepdims=True)
    e1 = i1 - N_GROUPS
    e2 = i2 - N_GROUPS
    tt = jnp.exp(m2 - m1)
    p1 = gp / (1.0 + tt)
    p2 = gp * tt / (1.0 + tt)
    comb = jnp.where(lane == e1, p1, 0.0) + jnp.where(lane == e2, p2, 0.0)

    lo = jnp.minimum(e1, e2) & (EXPERTS_PER_GROUP - 1)
    hi = jnp.maximum(e1, e2) & (EXPERTS_PER_GROUP - 1)
    pair = ((lo * (2 * EXPERTS_PER_GROUP - 1 - lo)) >> 1) + (hi - lo - 1)
    cls = gi * N_PAIRS + pair
    onehot = lane == cls
    row = lax.broadcasted_iota(jnp.int32, (tm, tm), 0)
    col = lax.broadcasted_iota(jnp.int32, (tm, tm), 1)
    before = jnp.where(col < row, 1.0, 0.0).astype(BF16)
    seen = _dot(before, jnp.where(onehot, 1.0, 0.0).astype(BF16)) + cnt_ref[0:1, :]
    rank = jnp.sum(jnp.where(onehot, seen, 0.0), axis=1, keepdims=True)
    cnt_ref[0:1, :] = cnt_ref[0:1, :] + jnp.sum(jnp.where(onehot, 1.0, 0.0), axis=0, keepdims=True)

    meta = comb + jnp.where(lane == META_CLS, cls.astype(F32), 0.0) \
        + jnp.where(lane == META_RANK, rank, 0.0)
    hx_ref[:, d:d + LANES] = meta


def _merge_route(x2d, oa, ob, oc, gates, woa, wob, woc, wout, gf, wrh, wrl, br, *, seq):
    n, d = x2d.shape
    tm = min(TOKEN_TILE, seq)
    row = lambda w: pl.BlockSpec((tm, w), lambda i: (i, 0))
    return pl.pallas_call(
        functools.partial(_merge_route_kernel, tm=tm, d=d),
        grid=(n // tm,),
        in_specs=[row(d), row(GLA_V), row(DIFF_V), row(CONV_WIDTH), row(N_BRANCH * d),
                  _const_spec(woa.shape), _const_spec(wob.shape), _const_spec(woc.shape),
                  _const_spec(wout.shape), _const_spec(gf.shape), _const_spec(wrh.shape),
                  _const_spec(wrl.shape), _const_spec(br.shape)],
        out_specs=(row(d), row(d + LANES), pl.BlockSpec((8, LANES), lambda i: (0, 0))),
        out_shape=(jax.ShapeDtypeStruct((n, d), F32), jax.ShapeDtypeStruct((n, d + LANES), F32),
                   jax.ShapeDtypeStruct((8, LANES), F32)),
        compiler_params=_params(("arbitrary",)),
        name="merge_route",
    )(x2d, oa, ob, oc, gates, woa, wob, woc, wout, gf, wrh, wrl, br)


def _expert_kernel(e1_ref, e2_ref, nv_ref, src_ref, dst_ref, hx_ref, wgu1_ref, wgu2_ref, wdn1_ref, wdn2_ref,
                   y_ref, *scratch, te, d, dexp, n_tok):
    nb = EXPERT_BUFS
    xbufs, ybufs, (gsem_ref, ssem_ref) = scratch[:nb], scratch[nb:2 * nb], scratch[2 * nb:]
    j = pl.program_id(0)
    nv = nv_ref[0]

    def gather_one(tile, r, par, priority=0):
        pltpu.async_copy(hx_ref.at[pl.ds(src_ref[tile * te + r], 1)], xbufs[par].at[pl.ds(r, 1)],
                         gsem_ref.at[par], priority=priority)

    def scatter_one(tile, r, par, priority=0):
        pltpu.async_copy(ybufs[par].at[pl.ds(r, 1)], y_ref.at[pl.ds(dst_ref[tile * te + r], 1)],
                         ssem_ref.at[par], priority=priority)

    def wait_gather(par):
        pltpu.make_async_copy(hx_ref.at[pl.ds(0, te)], xbufs[par], gsem_ref.at[par]).wait()

    def wait_scatter(par):
        pltpu.make_async_copy(ybufs[par], y_ref.at[pl.ds(0, te)], ssem_ref.at[par]).wait()

    def expert(par, which):
        e_ref, wgu_ref, wdn_ref = ((e1_ref, wgu1_ref, wdn1_ref), (e2_ref, wgu2_ref, wdn2_ref))[which]
        xb = xbufs[par]
        lane = lax.broadcasted_iota(jnp.int32, (te, LANES), 1)
        gu = _dot(xb[:, 0:d].astype(BF16), wgu_ref[...])
        gate = gu[:, :dexp]
        act = (gate * _sigmoid(gate) * gu[:, dexp:]).astype(BF16)
        wcol = jnp.sum(jnp.where(lane == e_ref[j], xb[:, d:d + LANES], 0.0), axis=1, keepdims=True)
        return wcol * _dot(act, wdn_ref[...])

    def compute(par):
        ybufs[par][...] = expert(par, 0) + expert(par, 1)

    def looped(fn, tile, par):
        def body(r, c):
            fn(tile, r, par)
            return c
        lax.fori_loop(0, te, body, 0)

    steady = jnp.logical_and(j >= nb, j + nb - 1 < nv)
    for par in range(nb):
        mine = (j % nb) == par
        ahead = (par + nb - 1) % nb

        @pl.when(jnp.logical_and(mine, steady))
        def _(par=par, ahead=ahead):
            wait_gather(par)
            wait_scatter(par)

            @pl.when(nv > 0)
            def _():
                ybufs[par][...] = expert(par, 0)
                for r in range(te):
                    gather_one(j + nb - 1, r, ahead, r % 2)

            @pl.when(nv > 1)
            def _():
                ybufs[par][...] += expert(par, 1)
                for r in range(te):
                    scatter_one(j, r, par, r % 2)

        @pl.when(jnp.logical_and(mine, jnp.logical_not(steady)))
        def _(par=par, ahead=ahead):
            if par == 0:
                @pl.when(j == 0)
                def _():
                    ybufs[0][...] = jnp.zeros((te, d), F32)
                    for k in range(nb):
                        fill = pltpu.make_async_copy(ybufs[0], y_ref.at[pl.ds(n_tok + k * te, te)],
                                                     ssem_ref.at[0])
                        fill.start()
                        fill.wait()
                    for k in range(nb - 1):
                        @pl.when(k < nv)
                        def _(k=k):
                            looped(gather_one, k, k)

            @pl.when(j < nv)
            def _():
                wait_gather(par)

            @pl.when(jnp.logical_and(j >= nb, j - nb < nv))
            def _():
                wait_scatter(par)

            @pl.when(j < nv)
            def _():
                compute(par)
                looped(scatter_one, j, par)

            @pl.when(j + nb - 1 < nv)
            def _():
                looped(gather_one, j + nb - 1, ahead)


def _experts(hx, wgu, wdn, e1, e2, nvalid, src, dst, *, te, n_tiles):
    n, dx = hx.shape
    d = dx - LANES
    dexp = wdn.shape[1]
    grid_spec = pltpu.PrefetchScalarGridSpec(
        num_scalar_prefetch=5,
        grid=(n_tiles + EXPERT_BUFS,),
        in_specs=[
            pl.BlockSpec(memory_space=pl.ANY),
            pl.BlockSpec((None, d, 2 * dexp), lambda j, e1, e2, nv, src, dst: (e1[j], 0, 0)),
            pl.BlockSpec((None, d, 2 * dexp), lambda j, e1, e2, nv, src, dst: (e2[j], 0, 0)),
            pl.BlockSpec((None, dexp, d), lambda j, e1, e2, nv, src, dst: (e1[j], 0, 0)),
            pl.BlockSpec((None, dexp, d), lambda j, e1, e2, nv, src, dst: (e2[j], 0, 0)),
        ],
        out_specs=pl.BlockSpec(memory_space=pl.ANY),
        scratch_shapes=([pltpu.VMEM((te, dx), F32)] * EXPERT_BUFS + [pltpu.VMEM((te, d), F32)] * EXPERT_BUFS
                        + [pltpu.SemaphoreType.DMA((EXPERT_BUFS,)), pltpu.SemaphoreType.DMA((EXPERT_BUFS,))]),
    )
    return pl.pallas_call(
        functools.partial(_expert_kernel, te=te, d=d, dexp=dexp, n_tok=n),
        grid_spec=grid_spec,
        out_shape=jax.ShapeDtypeStruct((n + EXPERT_BUFS * te, d), F32),
        compiler_params=_params(("arbitrary",)),
        name="experts",
    )(e1, e2, nvalid, src, dst, hx, wgu, wgu, wdn, wdn)


_PAIR_LO = np.array([lo for lo in range(EXPERTS_PER_GROUP) for hi in range(lo + 1, EXPERTS_PER_GROUP)],
                    np.int32)
_PAIR_HI = np.array([hi for lo in range(EXPERTS_PER_GROUP) for hi in range(lo + 1, EXPERTS_PER_GROUP)],
                    np.int32)


def _dispatch_tables(hx, cnt, *, te, n_tiles):
    n = hx.shape[0]
    d = hx.shape[1] - LANES
    cls = hx[:, d + META_CLS].astype(jnp.int32)
    rank = hx[:, d + META_RANK].astype(jnp.int32)
    counts = cnt[0, :N_CLASSES].astype(jnp.int32)
    tiles_c = (counts + te - 1) // te
    tile_end = jnp.cumsum(tiles_c)
    tile_start = tile_end - tiles_c
    start_of = jnp.dot(jax.nn.one_hot(cls, N_CLASSES, dtype=F32), tile_start.astype(F32),
                       precision=lax.Precision.HIGHEST)
    dest = start_of.astype(jnp.int32) * te + rank
    slot = jnp.arange(n_tiles * te, dtype=jnp.int32)
    tok = jnp.full((n_tiles * te,), -1, jnp.int32).at[dest].set(jnp.arange(n, dtype=jnp.int32))
    src = jnp.maximum(tok, 0)
    dst = jnp.where(tok < 0, n + ((slot // te) % EXPERT_BUFS) * te + slot % te, tok)
    tile_ids = jnp.arange(n_tiles + EXPERT_BUFS, dtype=jnp.int32)
    tcls = jnp.minimum(jnp.sum((tile_ids[:, None] >= tile_end[None, :]).astype(jnp.int32), axis=1),
                       N_CLASSES - 1)
    grp = tcls // N_PAIRS
    pr = tcls % N_PAIRS
    e1 = grp * EXPERTS_PER_GROUP + jnp.asarray(_PAIR_LO)[pr]
    e2 = grp * EXPERTS_PER_GROUP + jnp.asarray(_PAIR_HI)[pr]
    return e1.astype(jnp.int32), e2.astype(jnp.int32), tile_end[-1:].astype(jnp.int32), src, dst


def _ple_kernel(x1_ref, y_ref, p_ref, wpg_ref, wple_ref, gfin_ref, o_ref, *, final):
    x2 = x1_ref[...] + y_ref[...]
    gate = _sigmoid(_dot(x2.astype(BF16), wpg_ref[...]))
    x3 = x2 + gate * _dot(p_ref[...].astype(BF16), wple_ref[...])
    o_ref[...] = _rms(x3, gfin_ref[...]) if final else x3


def _ple(x1, y, p2d, wpg, wple, gfin, *, seq, final):
    n, d = x1.shape
    tm = min(WIDE_TOKEN_TILE, seq)
    row = lambda w: pl.BlockSpec((tm, w), lambda i: (i, 0))
    return pl.pallas_call(
        functools.partial(_ple_kernel, final=final),
        grid=(n // tm,),
        in_specs=[row(d), row(d), row(p2d.shape[1]), _const_spec(wpg.shape), _const_spec(wple.shape),
                  _const_spec(gfin.shape)],
        out_specs=row(d),
        out_shape=jax.ShapeDtypeStruct((n, d), F32),
        compiler_params=_params(("arbitrary",)),
        name="ple",
    )(x1, y, p2d, wpg, wple, gfin)


def _pad_cols(w, width):
    return jnp.pad(w, ((0, 0), (0, width - w.shape[1])))


def _diff_constants(seq):
    slopes = [2.0 ** (-8.0 * (hd + 1.0) / DIFF_HEADS) for hd in range(DIFF_HEADS)]
    qc = np.zeros((1, 2 * DIFF_HEADS * LANES), np.float32)
    for hd in range(DIFF_HEADS):
        for mp in range(2):
            for dgt in range(N_POS_DIGITS):
                coef = slopes[hd] * float(POS_BASE ** dgt)
                mant = math.frexp(coef)[0] * 256.0
                assert mant == int(mant), "ALiBi coefficient must be exact in bf16"
                qc[0, (2 * hd + mp) * LANES + DIFF_DH + dgt] = coef
    assert seq <= POS_BASE ** N_POS_DIGITS
    pos = np.arange(seq)
    pf = np.zeros((seq, LANES), np.float32)
    for dgt in range(N_POS_DIGITS):
        pf[:, DIFF_DH + dgt] = (pos // (POS_BASE ** dgt)) % POS_BASE
    vc = np.zeros((1, LANES), np.float32)
    vc[0, 0] = 1.0
    iqk = np.zeros((2 * DIFF_HEADS * LANES, LANES), np.float32)
    iall = np.zeros((2 * DIFF_HEADS * LANES, LANES), np.float32)
    for hm in range(2 * DIFF_HEADS):
        iqk[hm * LANES:hm * LANES + DIFF_DH, hm] = 1.0
        iall[hm * LANES:(hm + 1) * LANES, hm] = 1.0
    return (jnp.asarray(qc), jnp.asarray(pf), jnp.asarray(vc), jnp.asarray(iqk, BF16),
            jnp.asarray(iall, BF16))


def kernel(x, p, g_mix, w_in, w_gla_a2, b_gla_a, g_gla_norm, w_o_gla, diff_lam, g_diff_norm, w_o_diff,
           conv_w, w_o_conv, w_out, g_ffn, w_router_group, b_router_group, w_router_expert,
           b_router_expert, w_expert_gate_up, w_expert_down, w_ple, w_ple_gate, g_final):
    batch, seq, d = x.shape
    depth = w_in.shape[0]
    n = batch * seq
    te = min(EXPERT_TILE, seq)
    n_tiles = n // te + N_CLASSES
    qc, pf, vc, iqk, iall = _diff_constants(seq)

    sizes = (GLA_QK, GLA_QK, GLA_V, GLA_V, GLA_RANK, DIFF_QK, DIFF_QK, DIFF_V,
             CONV_WIDTH, CONV_WIDTH, CONV_WIDTH, N_BRANCH * d)
    offs = np.concatenate([[0], np.cumsum(sizes)]).tolist()
    assert offs[-1] == w_in.shape[2]

    x2d = x.reshape(n, d)
    for layer in range(depth):
        wi = w_in[layer]
        seg = lambda a, b: wi[:, offs[a]:offs[b]]
        wgla = seg(0, 4).astype(BF16)
        wga = _pad_cols(seg(4, 5), LANES).astype(BF16)
        w2 = jnp.pad(w_gla_a2[layer], ((0, LANES - GLA_RANK), (0, 0))).astype(BF16)
        wq, wk, wv = seg(5, 6).astype(BF16), seg(6, 7).astype(BF16), seg(7, 8).astype(BF16)
        wconv = seg(8, 11).astype(BF16)
        wgates = seg(11, 12).astype(BF16)
        lam_init = 0.8 - 0.6 * math.exp(-0.3 * layer)

        gq, gk, la, gv, sg, yc, gates, dq, dkt, dv, stats = _inproj(
            x2d, g_mix[layer][None, :],
            (wgla, wga, w2, b_gla_a[layer][None, :], wconv, conv_w[layer], wgates),
            (wq, qc, wk, pf, wv, vc, iqk, iall), batch=batch, seq=seq)
        oa = _gla(gq, gk, la, gv, sg, g_gla_norm[layer][None, :], batch=batch, seq=seq)
        ob = _attn(dq, dkt, dv, stats, diff_lam[layer], g_diff_norm[layer][None, :],
                   batch=batch, seq=seq, lam_init=lam_init)

        wr = _pad_cols(jnp.concatenate([w_router_group[layer], w_router_expert[layer]], axis=1), LANES)
        wrh = wr.astype(BF16)
        wrl = jnp.concatenate([wrh, (wr - wrh.astype(F32)).astype(BF16)], axis=1)
        br = _pad_cols(jnp.concatenate([b_router_group[layer], b_router_expert[layer]])[None, :], LANES)
        x1, hx, cnt = _merge_route(
            x2d, oa, ob, yc, gates, w_o_gla[layer].astype(BF16), w_o_diff[layer].astype(BF16),
            w_o_conv[layer].astype(BF16), w_out[layer].astype(BF16), g_ffn[layer][None, :],
            wrh, wrl, br, seq=seq)

        e1, e2, nvalid, src, dst = _dispatch_tables(hx, cnt, te=te, n_tiles=n_tiles)
        y = _experts(hx, w_expert_gate_up[layer].astype(BF16), w_expert_down[layer].astype(BF16),
                     e1, e2, nvalid, src, dst, te=te, n_tiles=n_tiles)

        x2d = _ple(x1, y, p[layer].reshape(n, p.shape[-1]), w_ple_gate[layer].astype(BF16),
                   w_ple[layer].astype(BF16), g_final[None, :], seq=seq,
                   final=(layer == depth - 1))
    return x2d.reshape(batch, seq, d)
```
